```python
import math
import jax, jax.numpy as jnp
from jax import lax
import numpy as np

D_MODEL = 2048
BATCH = 1
SEQ = 8192
DEPTH = 1

N_META = 16
RWKV_HEADS = 16
RWKV_HEAD_DIM = 64
RWKV_WIDTH = RWKV_HEADS * RWKV_HEAD_DIM
DECAY_LORA = 64
AAA_LORA = 64
GATE_LORA = 160
RWKV_SIZES = (RWKV_WIDTH, RWKV_WIDTH, RWKV_WIDTH, DECAY_LORA, AAA_LORA, GATE_LORA)
RWKV_COLS = sum(RWKV_SIZES)
RWKV_LN_EPS = 64e-5
DIFF_HEADS = 8
DIFF_QK_DIM = 64
DIFF_V_DIM = 2 * DIFF_QK_DIM
DIFF_WIDTH = DIFF_HEADS * DIFF_V_DIM
DIFF_COLS = 3 * DIFF_WIDTH
Q_BLOCK = 128
MIX_WIDTH = RWKV_WIDTH + DIFF_WIDTH
IN_COLS = RWKV_COLS + DIFF_COLS
D_FF = 5632
CONV_WIDTH = 3
NORM_EPS = 1e-6
SUBLN_EPS = 1e-5

kernel_name = "hymba_rwkv7_diffattn_convglu_block"


def rmsnorm(x, gain, eps):
    x32 = x.astype(jnp.float32)
    y = x32 * lax.rsqrt(jnp.mean(x32 * x32, axis=-1, keepdims=True) + eps)
    return (y * gain.astype(jnp.float32)).astype(x.dtype)


def rwkv7_mixer(p, mu, w0, w2, a0, a2, g2, k_k, k_a, r_k, lnx_w, lnx_b):
    B, T, _ = p.shape
    H, N = RWKV_HEADS, RWKV_HEAD_DIM
    prev = jnp.pad(p, ((0, 0), (1, 0), (0, 0)))[:, :-1]
    p = p + (prev - p) * mu
    cuts = list(np.cumsum(RWKV_SIZES)[:-1])
    r, k, v, wd, ad, gd = jnp.split(p, cuts, axis=-1)
    w = -jax.nn.softplus(-(w0 + jnp.tanh(wd) @ w2)) - 0.5
    a = jax.nn.sigmoid(a0 + ad @ a2)
    g = jax.nn.sigmoid(gd) @ g2
    hd = lambda t: t.reshape(B, T, H, N).astype(jnp.float32)
    kk = hd(k * k_k)
    kk = kk / jnp.maximum(jnp.linalg.norm(kk, axis=-1, keepdims=True), 1e-12)
    k = k * (1.0 + (a - 1.0) * k_a)
    r_h, k_h, v_h, a_h = hd(r), hd(k), hd(v), hd(a)
    decay = jnp.exp(-jnp.exp(hd(w)))
    tm = lambda t: jnp.swapaxes(t, 0, 1)

    def step(S, inp):
        r_t, d_t, k_t, v_t, kk_t, a_t = inp
        sa = jnp.einsum('bhij,bhj->bhi', S, -kk_t)
        S = (S * d_t[:, :, None, :]
             + sa[..., None] * (kk_t * a_t)[:, :, None, :]
             + v_t[..., None] * k_t[:, :, None, :])
        return S, jnp.einsum('bhij,bhj->bhi', S, r_t)

    S0 = jnp.zeros((B, H, N, N), jnp.float32)
    _, y = lax.scan(step, S0, (tm(r_h), tm(decay), tm(k_h), tm(v_h), tm(kk), tm(a_h)))
    y = jnp.swapaxes(y, 0, 1)
    mean = jnp.mean(y, axis=-1, keepdims=True)
    var = jnp.mean(jnp.square(y - mean), axis=-1, keepdims=True)
    y = (y - mean) * lax.rsqrt(var + RWKV_LN_EPS)
    y = y * lnx_w.reshape(H, N).astype(jnp.float32) + lnx_b.reshape(H, N).astype(jnp.float32)
    bonus = jnp.sum(r_h * k_h * r_k.astype(jnp.float32), axis=-1, keepdims=True) * v_h
    y = (y + bonus).reshape(B, T, RWKV_WIDTH).astype(p.dtype)
    return y * g


def diff_attention_mixer(p, q_gain, k_gain, lam_q1, lam_k1, lam_q2, lam_k2, subln_w, lambda_init):
    B, T, _ = p.shape
    H, D, E = DIFF_HEADS, DIFF_QK_DIM, DIFF_V_DIM
    q, k, v = jnp.split(p, 3, axis=-1)
    q = rmsnorm(q.reshape(B, T, H, 2, D), q_gain, NORM_EPS)
    k = rmsnorm(k.reshape(B, T, H, 2, D), k_gain, NORM_EPS)
    v = v.reshape(B, T, H, E)
    f32 = lambda t: t.astype(jnp.float32)
    lam = (jnp.exp(jnp.sum(f32(lam_q1) * f32(lam_k1))) - jnp.exp(jnp.sum(f32(lam_q2) * f32(lam_k2)))
           + lambda_init)
    n_blk = -(-T // Q_BLOCK)
    T_pad = n_blk * Q_BLOCK
    padt = lambda t: jnp.pad(t, [(0, 0), (0, T_pad - T)] + [(0, 0)] * (t.ndim - 2))
    q, k, v = padt(q), padt(k), padt(v)
    key_pos = jnp.arange(T_pad)
    scale = D ** -0.5

    def block(i):
        start = i * Q_BLOCK
        q_blk = lax.dynamic_slice_in_dim(q, start, Q_BLOCK, axis=1)
        s = jnp.einsum('bqhcd,bkhcd->bhcqk', q_blk, k).astype(jnp.float32) * scale
        q_pos = start + jnp.arange(Q_BLOCK)
        mask = key_pos[None, :] <= q_pos[:, None]
        prob = jax.nn.softmax(jnp.where(mask, s, -jnp.inf), axis=-1)
        attn = prob[:, :, 0] - lam * prob[:, :, 1]
        return jnp.einsum('bhqk,bkhe->bqhe', attn, v.astype(jnp.float32))

    o = lax.map(block, jnp.arange(n_blk))
    o = jnp.swapaxes(o, 0, 1).reshape(B, T_pad, H, E)[:, :T]
    o = rmsnorm(o, subln_w, SUBLN_EPS) * (1.0 - lambda_init)
    return o.reshape(B, T, DIFF_WIDTH).astype(p.dtype)


def conv_glu_ffn(h, w_up, conv_w, conv_b, w_down):
    u = h @ w_up
    C = u.shape[-1]
    u = lax.conv_general_dilated(u, conv_w[:, None, :], window_strides=(1,),
                                 padding=[(CONV_WIDTH - 1, 0)],
                                 dimension_numbers=('NWC', 'WIO', 'NWC'),
                                 feature_group_count=C) + conv_b
    gate, up = jnp.split(u, 2, axis=-1)
    return (jax.nn.silu(gate) * up) @ w_down


def setup_inputs(seed: int = 0) -> dict:
    key = jax.random.key(seed)
    ks = jax.random.split(key, 32)
    L, D = DEPTH, D_MODEL
    nrm = lambda k, shape, s: jax.random.normal(k, shape, jnp.float32) * s
    return {
        "x": nrm(ks[0], (BATCH, SEQ, D), 1.0),
        "meta": nrm(ks[1], (N_META, D), 1.0),
        "norm1_g": 1.0 + nrm(ks[2], (L, D), 0.02),
        "w_in": nrm(ks[3], (L, D, IN_COLS), D ** -0.5),
        "mu": jax.random.uniform(ks[4], (L, RWKV_COLS), jnp.float32, 0.0, 1.0),
        "w0": jax.random.uniform(ks[5], (L, RWKV_WIDTH), jnp.float32, -5.0, 1.0),
        "w2": nrm(ks[6], (L, DECAY_LORA, RWKV_WIDTH), 0.1),
        "a0": nrm(ks[7], (L, RWKV_WIDTH), 0.1),
        "a2": nrm(ks[8], (L, AAA_LORA, RWKV_WIDTH), 0.1),
        "g2": nrm(ks[9], (L, GATE_LORA, RWKV_WIDTH), GATE_LORA ** -0.5),
        "k_k": 0.85 + nrm(ks[10], (L, RWKV_WIDTH), 0.05),
        "k_a": 1.0 + nrm(ks[11], (L, RWKV_WIDTH), 0.05),
        "r_k": nrm(ks[12], (L, RWKV_HEADS, RWKV_HEAD_DIM), 0.1),
        "lnx_w": 1.0 + nrm(ks[13], (L, RWKV_WIDTH), 0.02),
        "lnx_b": nrm(ks[14], (L, RWKV_WIDTH), 0.02),
        "q_gain": 1.0 + nrm(ks[15], (L, DIFF_QK_DIM), 0.02),
        "k_gain": 1.0 + nrm(ks[16], (L, DIFF_QK_DIM), 0.02),
        "lam_q1": nrm(ks[17], (L, DIFF_QK_DIM), 0.1),
        "lam_k1": nrm(ks[18], (L, DIFF_QK_DIM), 0.1),
        "lam_q2": nrm(ks[19], (L, DIFF_QK_DIM), 0.1),
        "lam_k2": nrm(ks[20], (L, DIFF_QK_DIM), 0.1),
        "subln_w": 1.0 + nrm(ks[21], (L, DIFF_V_DIM), 0.02),
        "w_out": nrm(ks[22], (L, MIX_WIDTH, D), MIX_WIDTH ** -0.5),
        "norm2_g": 1.0 + nrm(ks[23], (L, D), 0.02),
        "w_up": nrm(ks[24], (L, D, 2 * D_FF), D ** -0.5),
        "conv_w": nrm(ks[25], (L, CONV_WIDTH, 2 * D_FF), CONV_WIDTH ** -0.5),
        "conv_b": nrm(ks[26], (L, 2 * D_FF), 0.02),
        "w_down": nrm(ks[27], (L, D_FF, D), D_FF ** -0.5),
    }


def reference(x, meta, norm1_g, w_in, mu, w0, w2, a0, a2, g2, k_k, k_a, r_k, lnx_w, lnx_b,
              q_gain, k_gain, lam_q1, lam_k1, lam_q2, lam_k2, subln_w, w_out,
              norm2_g, w_up, conv_w, conv_b, w_down):
    B = x.shape[0]
    h = jnp.concatenate([jnp.broadcast_to(meta[None].astype(x.dtype), (B, N_META, D_MODEL)), x], axis=1)
    for l in range(DEPTH):
        lambda_init = 0.8 - 0.6 * math.exp(-0.3 * l)
        hn = rmsnorm(h, norm1_g[l], NORM_EPS)
        proj = hn @ w_in[l]
        y_a = rwkv7_mixer(proj[..., :RWKV_COLS], mu[l], w0[l], w2[l], a0[l], a2[l], g2[l],
                          k_k[l], k_a[l], r_k[l], lnx_w[l], lnx_b[l])
        y_b = diff_attention_mixer(proj[..., RWKV_COLS:], q_gain[l], k_gain[l], lam_q1[l], lam_k1[l],
                                   lam_q2[l], lam_k2[l], subln_w[l], lambda_init)
        h = h + jnp.concatenate([y_a, y_b], axis=-1) @ w_out[l]
        h = h + conv_glu_ffn(rmsnorm(h, norm2_g[l], NORM_EPS), w_up[l], conv_w[l], conv_b[l], w_down[l])
    return h[:, N_META:]
```

```python
import functools
import math

import jax
import jax.numpy as jnp
import numpy as np
from jax import lax
from jax.experimental import pallas as pl
from jax.experimental.pallas import tpu as pltpu

F32 = jnp.float32
BF16 = jnp.bfloat16

N_META = 16
RWKV_HEADS = 16
RWKV_HEAD_DIM = 64
RWKV_WIDTH = RWKV_HEADS * RWKV_HEAD_DIM
DECAY_LORA = 64
AAA_LORA = 64
GATE_LORA = 160
RWKV_LN_EPS = 64e-5
DIFF_HEADS = 8
DIFF_QK_DIM = 64
DIFF_V_DIM = 128
DIFF_WIDTH = DIFF_HEADS * DIFF_V_DIM
CONV_WIDTH = 3
NORM_EPS = 1e-6
SUBLN_EPS = 1e-5

LANES = 128
ROW_TILE = 512
CHUNK = 64
RWKV_ROWS = 2 * CHUNK
LORA_PAD = (128, 128, 256)
LORA_COLS = sum(LORA_PAD)
PROJ_COLS = 3 * DIFF_WIDTH + 3 * RWKV_WIDTH + LORA_COLS
VMEM_LIMIT = 56 * 1024 * 1024


def _cparams(sem):
    return pltpu.CompilerParams(dimension_semantics=sem, vmem_limit_bytes=VMEM_LIMIT)


def _split_dot(a, b_bf16, parts):
    acc = None
    rem = a
    for _ in range(parts):
        hi = rem.astype(BF16)
        term = jnp.dot(hi, b_bf16, preferred_element_type=F32)
        acc = term if acc is None else acc + term
        rem = rem - hi.astype(F32)
    return acc


def _dot3(a, b):
    ah = a.astype(BF16)
    al = (a - ah.astype(F32)).astype(BF16)
    bh = b.astype(BF16)
    bl = (b - bh.astype(F32)).astype(BF16)
    d = functools.partial(jnp.dot, preferred_element_type=F32)
    return d(ah, bh) + (d(ah, bl) + d(al, bh))


def _group_ones(n, group):
    r = lax.broadcasted_iota(jnp.int32, (n, n), 0) // group
    c = lax.broadcasted_iota(jnp.int32, (n, n), 1) // group
    return (r == c).astype(BF16)


def _group_sum(x, group):
    ones = _group_ones(LANES, group)
    cols = [
        _split_dot(x[:, c:c + LANES], ones, 2)
        for c in range(0, x.shape[1], LANES)
    ]
    return jnp.concatenate(cols, axis=1)


def _proj_body(x_ref, g_ref, w_ref, o_ref, hn_ref):
    @pl.when(pl.program_id(1) == 0)
    def _():
        x = x_ref[...]
        ms = jnp.mean(x * x, axis=-1, keepdims=True)
        hn_ref[...] = (x * lax.rsqrt(ms + NORM_EPS) * g_ref[...]).astype(BF16)

    o_ref[...] = jnp.dot(hn_ref[...], w_ref[...], preferred_element_type=F32)


def _proj(h, gain, w, col_tile):
    tp, d = h.shape
    n = w.shape[1]
    return pl.pallas_call(
        _proj_body,
        grid=(tp // ROW_TILE, n // col_tile),
        in_specs=[
            pl.BlockSpec((ROW_TILE, d), lambda i, j: (i, 0)),
            pl.BlockSpec((1, d), lambda i, j: (0, 0)),
            pl.BlockSpec((d, col_tile), lambda i, j: (0, j)),
        ],
        out_specs=pl.BlockSpec((ROW_TILE, col_tile), lambda i, j: (i, j)),
        out_shape=jax.ShapeDtypeStruct((tp, n), F32),
        scratch_shapes=[pltpu.VMEM((ROW_TILE, d), BF16)],
        compiler_params=_cparams(("arbitrary", "arbitrary")),
        name="proj",
    )(h, gain, w)


def _softplus(z):
    return jnp.maximum(z, 0.0) + jnp.log(1.0 + jnp.exp(-jnp.abs(z)))


def _sigmoid(z):
    return 1.0 / (1.0 + jnp.exp(-z))


def _rwkv_pre_body(r_ref, k_ref, v_ref, l_ref, mu_ref, mul_ref, w0_ref, w2_ref, a0_ref, a2_ref,
                   g2_ref, kk_ref, ka_ref, rk_ref,
                   at_ref, rt_ref, bt_ref, kt_ref, vo_ref, bht_ref, kht_ref, gct_ref, g_ref, bonus_ref,
                   last_ref, lastl_ref):
    i = pl.program_id(0)

    @pl.when(i == 0)
    def _():
        last_ref[...] = jnp.zeros_like(last_ref)
        lastl_ref[...] = jnp.zeros_like(lastl_ref)

    rows = r_ref.shape[0]
    row = lax.broadcasted_iota(jnp.int32, (rows, 1), 0)

    def shift_mix(p, last_row, mu):
        prev = jnp.where(row == 0, last_row, pltpu.roll(p, 1, axis=0))
        return p + (prev - p) * mu

    w = RWKV_WIDTH
    r_raw, k_raw, v_raw, l_raw = r_ref[...], k_ref[...], v_ref[...], l_ref[...]
    r = shift_mix(r_raw, last_ref[0:1, 0:w], mu_ref[:, 0:w])
    k = shift_mix(k_raw, last_ref[0:1, w:2 * w], mu_ref[:, w:2 * w])
    v = shift_mix(v_raw, last_ref[0:1, 2 * w:3 * w], mu_ref[:, 2 * w:3 * w])
    lo = shift_mix(l_raw, lastl_ref[0:1, :], mul_ref[...])
    last_ref[0:1, 0:w] = r_raw[rows - 1:rows]
    last_ref[0:1, w:2 * w] = k_raw[rows - 1:rows]
    last_ref[0:1, 2 * w:3 * w] = v_raw[rows - 1:rows]
    lastl_ref[0:1, :] = l_raw[rows - 1:rows]

    c0, c1 = LORA_PAD[0], LORA_PAD[0] + LORA_PAD[1]
    wd, ad, gd = lo[:, 0:c0], lo[:, c0:c1], lo[:, c1:]
    wlog = -_softplus(-(w0_ref[...] + _dot3(jnp.tanh(wd), w2_ref[...]))) - 0.5
    logd = -jnp.exp(wlog)
    a = _sigmoid(a0_ref[...] + _dot3(ad, a2_ref[...]))
    g_ref[...] = _dot3(_sigmoid(gd), g2_ref[...])

    kk = k * kk_ref[...]
    kk = kk / jnp.maximum(jnp.sqrt(_group_sum(kk * kk, RWKV_HEAD_DIM)), 1e-12)
    k = k * (1.0 + (a - 1.0) * ka_ref[...])
    b = kk * a
    bonus_ref[...] = _group_sum(r * k * rk_ref[...], RWKV_HEAD_DIM) * v

    tr = lax.broadcasted_iota(jnp.int32, (rows, rows), 0)
    tc = lax.broadcasted_iota(jnp.int32, (rows, rows), 1)
    same = (tr // CHUNK) == (tc // CHUNK)
    cum = _split_dot_lhs((same & (tc <= tr)).astype(BF16), logd)
    tot = _split_dot_lhs(same.astype(BF16), logd)

    e_prev = jnp.exp(cum - logd)
    e_cum = jnp.exp(cum)
    e_neg = jnp.exp(-cum)
    e_rem = jnp.exp(tot - cum)
    at_ref[...] = (-kk * e_prev).astype(BF16)
    rt_ref[...] = (r * e_cum).astype(BF16)
    bt_ref[...] = (b * e_neg).astype(BF16)
    kt_ref[...] = (k * e_neg).astype(BF16)
    vo_ref[...] = v.astype(BF16)
    bht_ref[...] = (b * e_rem).T.astype(BF16)
    kht_ref[...] = (k * e_rem).T.astype(BF16)
    gct_ref[...] = jnp.exp(tot).T


def _split_dot_lhs(a_bf16, b):
    acc = None
    rem = b
    for _ in range(3):
        hi = rem.astype(BF16)
        term = jnp.dot(a_bf16, hi, preferred_element_type=F32)
        acc = term if acc is None else acc + term
        rem = rem - hi.astype(F32)
    return acc


def _rwkv_pre(p, mu, mul, w0, w2, a0, a2, g2, k_k, k_a, r_k):
    tp = p.shape[0]
    w = RWKV_WIDTH
    rows = RWKV_ROWS
    nqkv = 3 * DIFF_WIDTH // w
    lora_blk = (3 * DIFF_WIDTH + 3 * w) // LORA_COLS

    def full(arr):
        return pl.BlockSpec(arr.shape, lambda i: (0, 0))

    tok = pl.BlockSpec((rows, w), lambda i: (i, 0))
    chan = pl.BlockSpec((w, rows), lambda i: (0, i))
    tok_bf = jax.ShapeDtypeStruct((tp, w), BF16)
    return pl.pallas_call(
        _rwkv_pre_body,
        grid=(tp // rows,),
        in_specs=[
            pl.BlockSpec((rows, w), lambda i: (i, nqkv)),
            pl.BlockSpec((rows, w), lambda i: (i, nqkv + 1)),
            pl.BlockSpec((rows, w), lambda i: (i, nqkv + 2)),
            pl.BlockSpec((rows, LORA_COLS), lambda i: (i, lora_blk)),
            full(mu), full(mul), full(w0), full(w2), full(a0), full(a2), full(g2),
            full(k_k), full(k_a), full(r_k),
        ],
        out_specs=[tok, tok, tok, tok, tok, chan, chan, chan, tok, tok],
        out_shape=[tok_bf, tok_bf, tok_bf, tok_bf, tok_bf,
                   jax.ShapeDtypeStruct((w, tp), BF16), jax.ShapeDtypeStruct((w, tp), BF16),
                   jax.ShapeDtypeStruct((w, tp), F32),
                   jax.ShapeDtypeStruct((tp, w), F32), jax.ShapeDtypeStruct((tp, w), F32)],
        scratch_shapes=[pltpu.VMEM((8, 3 * w), F32), pltpu.VMEM((8, LORA_COLS), F32)],
        compiler_params=_cparams(("arbitrary",)),
        name="rwkv_pre",
    )(p, p, p, p, mu, mul, w0, w2, a0, a2, g2, k_k, k_a, r_k)


def _rwkv_chunk_body(at_ref, rt_ref, bt_ref, kt_ref, v_ref, bht_ref, kht_ref, gct_ref, g_ref, bonus_ref,
                     lnw_ref, lnb_ref, y_ref, state_ref, yn_ref):
    n = RWKV_HEAD_DIM
    c = CHUNK

    @pl.when(pl.program_id(0) == 0)
    def _():
        state_ref[...] = jnp.zeros_like(state_ref)

    tr = lax.broadcasted_iota(jnp.int32, (2 * c, 2 * c), 0)
    tc = lax.broadcasted_iota(jnp.int32, (2 * c, 2 * c), 1) % c
    mask = ((tr < c) & (tc < tr)) | ((tr >= c) & (tc <= tr - c))
    er = lax.broadcasted_iota(jnp.int32, (c, c), 0)
    ec = lax.broadcasted_iota(jnp.int32, (c, c), 1)
    eye = (er == ec).astype(F32)
    zeros_cn = jnp.zeros((c, n), BF16)
    dot = functools.partial(jnp.dot, preferred_element_type=F32)
    bf = lambda t: t.astype(BF16)

    for ci in range(at_ref.shape[0] // c):
        rs = slice(ci * c, (ci + 1) * c)
        for h in range(RWKV_HEADS):
            cs = slice(h * n, (h + 1) * n)
            lhs = jnp.concatenate([at_ref[rs, cs], rt_ref[rs, cs]], axis=0)
            rhs = jnp.concatenate([bt_ref[rs, cs], kt_ref[rs, cs]], axis=0)
            w = lax.dot_general(lhs, rhs, (((1,), (1,)), ((), ())), preferred_element_type=F32)
            w = jnp.where(mask, w, 0.0)
            top, bot = w[:c], w[c:]
            a_ab = top[:, :c]
            h0 = state_ref[h]
            x0 = dot(lhs, bf(h0))
            v = v_ref[rs, cs]
            sys_rhs = x0[:c] + dot(bf(top), jnp.concatenate([zeros_cn, v], axis=0))
            inv = eye + a_ab
            ap = a_ab
            for _ in range(int(math.log2(c)) - 1):
                apb = bf(ap)
                ap = dot(apb, apb)
                inv = inv + dot(bf(inv), bf(ap))
            u = dot(bf(inv), bf(sys_rhs))
            uv = jnp.concatenate([bf(u), v], axis=0)
            y = x0[c:] + dot(bf(bot), uv)
            bkt = jnp.concatenate([bht_ref[cs, rs], kht_ref[cs, rs]], axis=1)
            state_ref[h] = gct_ref[cs, rs] * h0 + dot(bkt, uv)
            mean = jnp.mean(y, axis=-1, keepdims=True)
            d = y - mean
            var = jnp.mean(d * d, axis=-1, keepdims=True)
            yn_ref[rs, cs] = d * lax.rsqrt(var + RWKV_LN_EPS)

    y_ref[...] = ((yn_ref[...] * lnw_ref[...] + lnb_ref[...] + bonus_ref[...]) * g_ref[...]).astype(BF16)


def _rwkv_chunks(at, rt, bt, kt, v, bht, kht, gct, g, bonus, lnw, lnb):
    tp, w = at.shape
    rows = RWKV_ROWS
    tok = pl.BlockSpec((rows, w), lambda i: (i, 0))
    chan = pl.BlockSpec((w, rows), lambda i: (0, i))
    vec = pl.BlockSpec((1, w), lambda i: (0, 0))
    return pl.pallas_call(
        _rwkv_chunk_body,
        grid=(tp // rows,),
        in_specs=[tok, tok, tok, tok, tok, chan, chan, chan, tok, tok, vec, vec],
        out_specs=tok,
        out_shape=jax.ShapeDtypeStruct((tp, w), BF16),
        scratch_shapes=[pltpu.VMEM((RWKV_HEADS, RWKV_HEAD_DIM, RWKV_HEAD_DIM), F32),
                        pltpu.VMEM((rows, w), F32)],
        compiler_params=_cparams(("arbitrary",)),
        name="rwkv_chunks",
    )(at, rt, bt, kt, v, bht, kht, gct, g, bonus, lnw, lnb)


def _attn_pre_body(q_ref, k_ref, v_ref, qg_ref, kg_ref, qz_ref, kn_ref, vb_ref):
    d = DIFF_QK_DIM

    def qk_norm(x, gain):
        ms = _group_sum(x * x, d) * (1.0 / d)
        return x * lax.rsqrt(ms + NORM_EPS) * gain

    q = qk_norm(q_ref[...], qg_ref[...]) * (d ** -0.5)
    first = (lax.broadcasted_iota(jnp.int32, q.shape, 1) % (2 * d)) < d
    qz_ref[0] = jnp.where(first, q, 0.0).astype(BF16)
    qz_ref[1] = jnp.where(first, 0.0, q).astype(BF16)
    kn_ref[...] = qk_norm(k_ref[...], kg_ref[...]).astype(BF16)
    vb_ref[...] = v_ref[...].astype(BF16)


def _attn_pre(p, q_gain, k_gain):
    tp = p.shape[0]
    w = DIFF_WIDTH
    vec = pl.BlockSpec((1, w), lambda i: (0, 0))
    blk = lambda j: pl.BlockSpec((ROW_TILE, w), lambda i: (i, j))
    return pl.pallas_call(
        _attn_pre_body,
        grid=(tp // ROW_TILE,),
        in_specs=[blk(0), blk(1), blk(2), vec, vec],
        out_specs=[pl.BlockSpec((2, ROW_TILE, w), lambda i: (0, i, 0)), blk(0), blk(0)],
        out_shape=[jax.ShapeDtypeStruct((2, tp, w), BF16), jax.ShapeDtypeStruct((tp, w), BF16),
                   jax.ShapeDtypeStruct((tp, w), BF16)],
        compiler_params=_cparams(("arbitrary",)),
        name="attn_pre",
    )(p, p, p, q_gain, k_gain)


def _flash_body(qi_ref, ki_ref, qz_ref, k_ref, v_ref, lam_ref, sw_ref, o_ref, m_ref, l_ref, acc_ref,
                *, first_key, lambda_init):
    step = pl.program_id(1)
    qi = qi_ref[step]
    ki = ki_ref[step]
    tq = o_ref.shape[0]
    tk = k_ref.shape[0]

    @pl.when(ki == 0)
    def _():
        m_ref[...] = jnp.full_like(m_ref, -jnp.inf)
        l_ref[...] = jnp.zeros_like(l_ref)
        acc_ref[...] = jnp.zeros_like(acc_ref)

    q = qz_ref[...].reshape(2 * tq, qz_ref.shape[2])
    s = lax.dot_general(q, k_ref[...], (((1,), (1,)), ((), ())), preferred_element_type=F32)

    def update(s):
        m_prev = m_ref[...]
        m_new = jnp.maximum(m_prev, jnp.max(s, axis=-1, keepdims=True))
        alpha = jnp.exp(m_prev - m_new)
        p = jnp.exp(s - m_new)
        l_ref[...] = alpha * l_ref[...] + jnp.sum(p, axis=-1, keepdims=True)
        acc_ref[...] = alpha * acc_ref[...] + jnp.dot(p.astype(BF16), v_ref[...],
                                                       preferred_element_type=F32)
        m_ref[...] = m_new

    needs_mask = (ki == qi) | (ki == 0)

    @pl.when(needs_mask)
    def _():
        q_pos = qi * tq + lax.broadcasted_iota(jnp.int32, (2 * tq, tk), 0) % tq
        k_pos = ki * tk + lax.broadcasted_iota(jnp.int32, (2 * tq, tk), 1)
        visible = (k_pos <= q_pos) & ((k_pos >= first_key) | (q_pos < first_key))
        update(jnp.where(visible, s, -jnp.inf))

    @pl.when(jnp.logical_not(needs_mask))
    def _():
        update(s)

    @pl.when(ki == qi)
    def _():
        lp = lam_ref[...]
        lam = (jnp.exp(jnp.sum(lp[0:1] * lp[1:2])) - jnp.exp(jnp.sum(lp[2:3] * lp[3:4])) + lambda_init)
        o = acc_ref[...] / l_ref[...]
        o = o[:tq] - lam * o[tq:]
        ms = jnp.mean(o * o, axis=-1, keepdims=True)
        o = o * lax.rsqrt(ms + SUBLN_EPS) * sw_ref[...] * (1.0 - lambda_init)
        o_ref[...] = o.astype(BF16)


def _flash(qz, kn, vb, lam_p, subln_w, first_key, lambda_init):
    _, tp, w = qz.shape
    e = DIFF_V_DIM
    nb = tp // ROW_TILE
    pairs = [(qi, ki) for qi in range(nb) for ki in range(qi + 1)]
    qi_tab = jnp.asarray(np.array([p[0] for p in pairs], np.int32))
    ki_tab = jnp.asarray(np.array([p[1] for p in pairs], np.int32))
    grid_spec = pltpu.PrefetchScalarGridSpec(
        num_scalar_prefetch=2,
        grid=(DIFF_HEADS, len(pairs)),
        in_specs=[
            pl.BlockSpec((2, ROW_TILE, e), lambda h, s, qt, kt: (0, qt[s], h)),
            pl.BlockSpec((ROW_TILE, e), lambda h, s, qt, kt: (kt[s], h)),
            pl.BlockSpec((ROW_TILE, e), lambda h, s, qt, kt: (kt[s], h)),
            pl.BlockSpec(lam_p.shape, lambda h, s, qt, kt: (0, 0)),
            pl.BlockSpec((1, e), lambda h, s, qt, kt: (0, 0)),
        ],
        out_specs=pl.BlockSpec((ROW_TILE, e), lambda h, s, qt, kt: (qt[s], h)),
        scratch_shapes=[pltpu.VMEM((2 * ROW_TILE, 1), F32), pltpu.VMEM((2 * ROW_TILE, 1), F32),
                        pltpu.VMEM((2 * ROW_TILE, e), F32)],
    )
    return pl.pallas_call(
        functools.partial(_flash_body, first_key=first_key, lambda_init=lambda_init),
        grid_spec=grid_spec,
        out_shape=jax.ShapeDtypeStruct((tp, w), BF16),
        compiler_params=_cparams(("arbitrary", "arbitrary")),
        name="diff_flash",
    )(qi_tab, ki_tab, qz, kn, vb, lam_p, subln_w)


def _out_proj_body(ya_ref, yb_ref, wa_ref, wb_ref, h_ref, g_ref, h2_ref, hn_ref):
    acc = jnp.dot(ya_ref[...], wa_ref[...], preferred_element_type=F32)
    acc = acc + jnp.dot(yb_ref[...], wb_ref[...], preferred_element_type=F32)
    h2 = h_ref[...] + acc
    h2_ref[...] = h2
    ms = jnp.mean(h2 * h2, axis=-1, keepdims=True)
    hn_ref[...] = (h2 * lax.rsqrt(ms + NORM_EPS) * g_ref[...]).astype(BF16)


def _out_proj(ya, yb, wa, wb, h, gain, row_tile):
    tp, d = h.shape
    rows = pl.BlockSpec((row_tile, d), lambda i: (i, 0))
    return pl.pallas_call(
        _out_proj_body,
        grid=(tp // row_tile,),
        in_specs=[
            pl.BlockSpec((row_tile, ya.shape[1]), lambda i: (i, 0)),
            pl.BlockSpec((row_tile, yb.shape[1]), lambda i: (i, 0)),
            pl.BlockSpec(wa.shape, lambda i: (0, 0)),
            pl.BlockSpec(wb.shape, lambda i: (0, 0)),
            rows,
            pl.BlockSpec((1, d), lambda i: (0, 0)),
        ],
        out_specs=[rows, rows],
        out_shape=[jax.ShapeDtypeStruct((tp, d), F32), jax.ShapeDtypeStruct((tp, d), BF16)],
        compiler_params=_cparams(("arbitrary",)),
        name="out_proj",
    )(ya, yb, wa, wb, h, gain)


def _ffn_body(x_ref, halo_ref, wg_ref, wu_ref, cwg_ref, cwu_ref, cbg_ref, cbu_ref, wd_ref, h2_ref,
              o_ref, acc_ref):
    j = pl.program_id(1)
    rows = x_ref.shape[0]
    hrows = halo_ref.shape[0]
    row = lax.broadcasted_iota(jnp.int32, (rows, 1), 0)

    def conv(w_ref, cw_ref, cb_ref):
        u = jnp.dot(x_ref[...], w_ref[...], preferred_element_type=F32)
        uh = jnp.dot(halo_ref[...], w_ref[...], preferred_element_type=F32)
        p1, p2 = uh[hrows - 1:hrows], uh[hrows - 2:hrows - 1]
        u1 = jnp.where(row == 0, p1, pltpu.roll(u, 1, axis=0))
        u2 = jnp.where(row == 0, p2, jnp.where(row == 1, p1, pltpu.roll(u, 2, axis=0)))
        cw = cw_ref[...]
        return u2 * cw[0:1] + u1 * cw[1:2] + u * cw[2:3] + cb_ref[...]

    gate = conv(wg_ref, cwg_ref, cbg_ref)
    up = conv(wu_ref, cwu_ref, cbu_ref)
    act = (gate * _sigmoid(gate) * up).astype(BF16)
    part = jnp.dot(act, wd_ref[...], preferred_element_type=F32)

    @pl.when(j == 0)
    def _():
        acc_ref[...] = part

    @pl.when(j > 0)
    def _():
        acc_ref[...] += part

    @pl.when(j == pl.num_programs(1) - 1)
    def _():
        o_ref[...] = h2_ref[...] + acc_ref[...]


def _ffn(hn, h2, w_up, conv_w, conv_b, w_down, first_block, n_out_rows, ff_tile):
    tp, d = hn.shape
    dff = w_down.shape[0]
    nf = dff // ff_tile
    halo = 16
    per = ROW_TILE // halo
    fb = first_block
    return pl.pallas_call(
        _ffn_body,
        grid=(n_out_rows // ROW_TILE, nf),
        in_specs=[
            pl.BlockSpec((ROW_TILE, d), lambda i, j: (i + fb, 0)),
            pl.BlockSpec((halo, d), lambda i, j: ((i + fb) * per - 1, 0)),
            pl.BlockSpec((d, ff_tile), lambda i, j: (0, j)),
            pl.BlockSpec((d, ff_tile), lambda i, j: (0, j + nf)),
            pl.BlockSpec((CONV_WIDTH, ff_tile), lambda i, j: (0, j)),
            pl.BlockSpec((CONV_WIDTH, ff_tile), lambda i, j: (0, j + nf)),
            pl.BlockSpec((1, ff_tile), lambda i, j: (0, j)),
            pl.BlockSpec((1, ff_tile), lambda i, j: (0, j + nf)),
            pl.BlockSpec((ff_tile, d), lambda i, j: (j, 0)),
            pl.BlockSpec((ROW_TILE, d), lambda i, j: (i + fb, 0)),
        ],
        out_specs=pl.BlockSpec((ROW_TILE, d), lambda i, j: (i, 0)),
        out_shape=jax.ShapeDtypeStruct((n_out_rows, d), F32),
        scratch_shapes=[pltpu.VMEM((ROW_TILE, d), F32)],
        compiler_params=_cparams(("arbitrary", "arbitrary")),
        name="conv_glu_ffn",
    )(hn, hn, w_up, w_up, conv_w, conv_w, conv_b, conv_b, w_down, h2)


def _pad_cols(a, width):
    return jnp.pad(a, ((0, 0), (0, width - a.shape[1])))


def _pad_rows(a, height):
    return jnp.pad(a, ((0, height - a.shape[0]), (0, 0)))


def _layer(h, first_key, lambda_init, norm1_g, w_in, mu, w0, w2, a0, a2, g2, k_k, k_a, r_k, lnx_w, lnx_b,
           q_gain, k_gain, lam_q1, lam_k1, lam_q2, lam_k2, subln_w, w_out, norm2_g, w_up, conv_w, conv_b,
           w_down):
    w = RWKV_WIDTH
    row = lambda a: a.reshape(1, -1)
    rwkv_cols = 3 * w + DECAY_LORA + AAA_LORA + GATE_LORA
    cuts = np.cumsum([3 * w, DECAY_LORA, AAA_LORA])
    wi_rkv, wi_wd, wi_ad, wi_gd = jnp.split(w_in[:, :rwkv_cols], cuts, axis=1)
    mu_rkv, mu_wd, mu_ad, mu_gd = jnp.split(row(mu), cuts, axis=1)
    pads = LORA_PAD
    w_proj = jnp.concatenate(
        [w_in[:, rwkv_cols:], wi_rkv, _pad_cols(wi_wd, pads[0]), _pad_cols(wi_ad, pads[1]),
         _pad_cols(wi_gd, pads[2])], axis=1).astype(BF16)
    mu_l = jnp.concatenate([_pad_cols(mu_wd, pads[0]), _pad_cols(mu_ad, pads[1]), _pad_cols(mu_gd, pads[2])],
                           axis=1)

    p = _proj(h, row(norm1_g), w_proj, PROJ_COLS // 4)

    at, rt, bt, kt, v, bht, kht, gct, g, bonus = _rwkv_pre(
        p, mu_rkv, mu_l, row(w0), _pad_rows(w2, pads[0]), row(a0), _pad_rows(a2, pads[1]),
        _pad_rows(g2, pads[2]), row(k_k), row(k_a), row(r_k))
    y_a = _rwkv_chunks(at, rt, bt, kt, v, bht, kht, gct, g, bonus, row(lnx_w), row(lnx_b))

    tile_heads = lambda a: jnp.tile(row(a), (1, 2 * DIFF_HEADS))
    qz, kn, vb = _attn_pre(p, tile_heads(q_gain), tile_heads(k_gain))
    lam_p = jnp.stack([lam_q1, lam_k1, lam_q2, lam_k2]).astype(F32)
    y_b = _flash(qz, kn, vb, lam_p, row(subln_w), first_key, lambda_init)

    w_out_b = w_out.astype(BF16)
    h2, hn2 = _out_proj(y_a, y_b, w_out_b[:w], w_out_b[w:], h, row(norm2_g), ROW_TILE // 2)

    n_x = h.shape[0] - ROW_TILE
    return _ffn(hn2, h2, w_up.astype(BF16), conv_w, row(conv_b), w_down.astype(BF16), 1, n_x, 512)


def kernel(x, meta, norm1_g, w_in, mu, w0, w2, a0, a2, g2, k_k, k_a, r_k, lnx_w, lnx_b, q_gain, k_gain,
           lam_q1, lam_k1, lam_q2, lam_k2, subln_w, w_out, norm2_g, w_up, conv_w, conv_b, w_down):
    batch, seq, d = x.shape
    depth = norm1_g.shape[0]
    assert depth == 1 and seq % ROW_TILE == 0 and meta.shape[0] == N_META
    front = ROW_TILE - N_META
    outs = []
    for bi in range(batch):
        h = jnp.concatenate([jnp.zeros((front, d), x.dtype), meta.astype(x.dtype), x[bi]], axis=0)
        lambda_init = 0.8 - 0.6 * math.exp(-0.3 * 0)
        outs.append(_layer(
            h, front, lambda_init, norm1_g[0], w_in[0], mu[0], w0[0], w2[0], a0[0], a2[0], g2[0], k_k[0],
            k_a[0], r_k[0], lnx_w[0], lnx_b[0], q_gain[0], k_gain[0], lam_q1[0], lam_k1[0], lam_q2[0],
            lam_k2[0], subln_w[0], w_out[0], norm2_g[0], w_up[0], conv_w[0], conv_b[0], w_down[0]))
    return jnp.stack(outs, axis=0)
```

```python
import functools
import math

import jax
import jax.numpy as jnp
import numpy as np
from jax import lax
from jax.experimental import pallas as pl
from jax.experimental.pallas import tpu as pltpu

F32 = jnp.float32
BF16 = jnp.bfloat16

N_META = 16
RWKV_HEADS = 16
RWKV_HEAD_DIM = 64
RWKV_WIDTH = RWKV_HEADS * RWKV_HEAD_DIM
DECAY_LORA = 64
AAA_LORA = 64
GATE_LORA = 160
RWKV_LN_EPS = 64e-5
DIFF_HEADS = 8
DIFF_QK_DIM = 64
DIFF_V_DIM = 128
DIFF_WIDTH = DIFF_HEADS * DIFF_V_DIM
CONV_WIDTH = 3
NORM_EPS = 1e-6
SUBLN_EPS = 1e-5

LANES = 128
ROW_TILE = 512
CHUNK = 64
RWKV_ROWS = 2 * CHUNK
LORA_PAD = (128, 128, 256)
LORA_COLS = sum(LORA_PAD)
PROJ_COLS = 3 * DIFF_WIDTH + 3 * RWKV_WIDTH + LORA_COLS
VMEM_LIMIT = 56 * 1024 * 1024


def _cparams(sem):
    return pltpu.CompilerParams(dimension_semantics=sem, vmem_limit_bytes=VMEM_LIMIT)


def _split_dot(a, b_bf16, parts):
    acc = None
    rem = a
    for _ in range(parts):
        hi = rem.astype(BF16)
        term = jnp.dot(hi, b_bf16, preferred_element_type=F32)
        acc = term if acc is None else acc + term
        rem = rem - hi.astype(F32)
    return acc


def _dot3(a, b):
    ah = a.astype(BF16)
    al = (a - ah.astype(F32)).astype(BF16)
    bh = b.astype(BF16)
    bl = (b - bh.astype(F32)).astype(BF16)
    d = functools.partial(jnp.dot, preferred_element_type=F32)
    return d(ah, bh) + (d(ah, bl) + d(al, bh))


def _group_ones(n, group):
    r = lax.broadcasted_iota(jnp.int32, (n, n), 0) // group
    c = lax.broadcasted_iota(jnp.int32, (n, n), 1) // group
    return (r == c).astype(BF16)


def _group_sum(x, group):
    ones = _group_ones(LANES, group)
    cols = [
        _split_dot(x[:, c:c + LANES], ones, 2)
        for c in range(0, x.shape[1], LANES)
    ]
    return jnp.concatenate(cols, axis=1)


def _proj_body(x_ref, g_ref, w_ref, o_ref, hn_ref):
    @pl.when(pl.program_id(1) == 0)
    def _():
        x = x_ref[...]
        ms = jnp.mean(x * x, axis=-1, keepdims=True)
        hn_ref[...] = (x * lax.rsqrt(ms + NORM_EPS) * g_ref[...]).astype(BF16)

    o_ref[...] = jnp.dot(hn_ref[...], w_ref[...], preferred_element_type=F32)


def _proj(h, gain, w, col_tile):
    tp, d = h.shape
    n = w.shape[1]
    return pl.pallas_call(
        _proj_body,
        grid=(tp // ROW_TILE, n // col_tile),
        in_specs=[
            pl.BlockSpec((ROW_TILE, d), lambda i, j: (i, 0)),
            pl.BlockSpec((1, d), lambda i, j: (0, 0)),
            pl.BlockSpec((d, col_tile), lambda i, j: (0, j)),
        ],
        out_specs=pl.BlockSpec((ROW_TILE, col_tile), lambda i, j: (i, j)),
        out_shape=jax.ShapeDtypeStruct((tp, n), F32),
        scratch_shapes=[pltpu.VMEM((ROW_TILE, d), BF16)],
        compiler_params=_cparams(("arbitrary", "arbitrary")),
        name="proj",
    )(h, gain, w)


def _softplus(z):
    return jnp.maximum(z, 0.0) + jnp.log(1.0 + jnp.exp(-jnp.abs(z)))


def _sigmoid(z):
    return 1.0 / (1.0 + jnp.exp(-z))


def _rwkv_pre_body(r_ref, k_ref, v_ref, l_ref, mu_ref, mul_ref, w0_ref, w2_ref, a0_ref, a2_ref,
                   g2_ref, kk_ref, ka_ref, rk_ref,
                   at_ref, rt_ref, bt_ref, kt_ref, vo_ref, bht_ref, kht_ref, gct_ref, g_ref, bonus_ref,
                   last_ref, lastl_ref):
    i = pl.program_id(0)

    @pl.when(i == 0)
    def _():
        last_ref[...] = jnp.zeros_like(last_ref)
        lastl_ref[...] = jnp.zeros_like(lastl_ref)

    rows = r_ref.shape[0]
    row = lax.broadcasted_iota(jnp.int32, (rows, 1), 0)

    def shift_mix(p, last_row, mu):
        prev = jnp.where(row == 0, last_row, pltpu.roll(p, 1, axis=0))
        return p + (prev - p) * mu

    w = RWKV_WIDTH
    r_raw, k_raw, v_raw, l_raw = r_ref[...], k_ref[...], v_ref[...], l_ref[...]
    r = shift_mix(r_raw, last_ref[0:1, 0:w], mu_ref[:, 0:w])
    k = shift_mix(k_raw, last_ref[0:1, w:2 * w], mu_ref[:, w:2 * w])
    v = shift_mix(v_raw, last_ref[0:1, 2 * w:3 * w], mu_ref[:, 2 * w:3 * w])
    lo = shift_mix(l_raw, lastl_ref[0:1, :], mul_ref[...])
    last_ref[0:1, 0:w] = r_raw[rows - 1:rows]
    last_ref[0:1, w:2 * w] = k_raw[rows - 1:rows]
    last_ref[0:1, 2 * w:3 * w] = v_raw[rows - 1:rows]
    lastl_ref[0:1, :] = l_raw[rows - 1:rows]

    c0, c1 = LORA_PAD[0], LORA_PAD[0] + LORA_PAD[1]
    wd, ad, gd = lo[:, 0:c0], lo[:, c0:c1], lo[:, c1:]
    wlog = -_softplus(-(w0_ref[...] + _dot3(jnp.tanh(wd), w2_ref[...]))) - 0.5
    logd = -jnp.exp(wlog)
    a = _sigmoid(a0_ref[...] + _dot3(ad, a2_ref[...]))
    g_ref[...] = _dot3(_sigmoid(gd), g2_ref[...])

    kk = k * kk_ref[...]
    kk = kk / jnp.maximum(jnp.sqrt(_group_sum(kk * kk, RWKV_HEAD_DIM)), 1e-12)
    k = k * (1.0 + (a - 1.0) * ka_ref[...])
    b = kk * a
    bonus_ref[...] = _group_sum(r * k * rk_ref[...], RWKV_HEAD_DIM) * v

    tr = lax.broadcasted_iota(jnp.int32, (rows, rows), 0)
    tc = lax.broadcasted_iota(jnp.int32, (rows, rows), 1)
    same = (tr // CHUNK) == (tc // CHUNK)
    cum = _split_dot_lhs((same & (tc <= tr)).astype(BF16), logd)
    tot = _split_dot_lhs(same.astype(BF16), logd)

    e_prev = jnp.exp(cum - logd)
    e_cum = jnp.exp(cum)
    e_neg = jnp.exp(-cum)
    e_rem = jnp.exp(tot - cum)
    at_ref[...] = (-kk * e_prev).astype(BF16)
    rt_ref[...] = (r * e_cum).astype(BF16)
    bt_ref[...] = (b * e_neg).astype(BF16)
    kt_ref[...] = (k * e_neg).astype(BF16)
    vo_ref[...] = v.astype(BF16)
    bht_ref[...] = (b * e_rem).T.astype(BF16)
    kht_ref[...] = (k * e_rem).T.astype(BF16)
    gct_ref[...] = jnp.exp(tot).T


def _split_dot_lhs(a_bf16, b):
    acc = None
    rem = b
    for _ in range(3):
        hi = rem.astype(BF16)
        term = jnp.dot(a_bf16, hi, preferred_element_type=F32)
        acc = term if acc is None else acc + term
        rem = rem - hi.astype(F32)
    return acc


def _rwkv_pre(p, mu, mul, w0, w2, a0, a2, g2, k_k, k_a, r_k):
    tp = p.shape[0]
    w = RWKV_WIDTH
    rows = RWKV_ROWS
    nqkv = 3 * DIFF_WIDTH // w
    lora_blk = (3 * DIFF_WIDTH + 3 * w) // LORA_COLS

    def full(arr):
        return pl.BlockSpec(arr.shape, lambda i: (0, 0))

    tok = pl.BlockSpec((rows, w), lambda i: (i, 0))
    chan = pl.BlockSpec((w, rows), lambda i: (0, i))
    tok_bf = jax.ShapeDtypeStruct((tp, w), BF16)
    return pl.pallas_call(
        _rwkv_pre_body,
        grid=(tp // rows,),
        in_specs=[
            pl.BlockSpec((rows, w), lambda i: (i, nqkv)),
            pl.BlockSpec((rows, w), lambda i: (i, nqkv + 1)),
            pl.BlockSpec((rows, w), lambda i: (i, nqkv + 2)),
            pl.BlockSpec((rows, LORA_COLS), lambda i: (i, lora_blk)),
            full(mu), full(mul), full(w0), full(w2), full(a0), full(a2), full(g2),
            full(k_k), full(k_a), full(r_k),
        ],
        out_specs=[tok, tok, tok, tok, tok, chan, chan, chan, tok, tok],
        out_shape=[tok_bf, tok_bf, tok_bf, tok_bf, tok_bf,
                   jax.ShapeDtypeStruct((w, tp), BF16), jax.ShapeDtypeStruct((w, tp), BF16),
                   jax.ShapeDtypeStruct((w, tp), F32),
                   jax.ShapeDtypeStruct((tp, w), F32), jax.ShapeDtypeStruct((tp, w), F32)],
        scratch_shapes=[pltpu.VMEM((8, 3 * w), F32), pltpu.VMEM((8, LORA_COLS), F32)],
        compiler_params=_cparams(("arbitrary",)),
        name="rwkv_pre",
    )(p, p, p, p, mu, mul, w0, w2, a0, a2, g2, k_k, k_a, r_k)


def _rwkv_chunk_body(at_ref, rt_ref, bt_ref, kt_ref, v_ref, bht_ref, kht_ref, gct_ref, g_ref, bonus_ref,
                     lnw_ref, lnb_ref, y_ref, state_ref, yn_ref):
    n = RWKV_HEAD_DIM
    c = CHUNK

    @pl.when(pl.program_id(0) == 0)
    def _():
        state_ref[...] = jnp.zeros_like(state_ref)

    tr = lax.broadcasted_iota(jnp.int32, (2 * c, 2 * c), 0)
    tc = lax.broadcasted_iota(jnp.int32, (2 * c, 2 * c), 1) % c
    mask = ((tr < c) & (tc < tr)) | ((tr >= c) & (tc <= tr - c))
    er = lax.broadcasted_iota(jnp.int32, (c, c), 0)
    ec = lax.broadcasted_iota(jnp.int32, (c, c), 1)
    eye = (er == ec).astype(F32)
    zeros_cn = jnp.zeros((c, n), BF16)
    dot = functools.partial(jnp.dot, preferred_element_type=F32)
    bf = lambda t: t.astype(BF16)

    heads = range(RWKV_HEADS)
    chunks = range(at_ref.shape[0] // c)
    units = [(ci, h) for ci in chunks for h in heads]

    def blk(ref, u):
        return ref[u[0] * c:(u[0] + 1) * c, u[1] * n:(u[1] + 1) * n]

    lhs = {u: jnp.concatenate([blk(at_ref, u), blk(rt_ref, u)], axis=0) for u in units}
    w = {}
    for u in units:
        rhs = jnp.concatenate([blk(bt_ref, u), blk(kt_ref, u)], axis=0)
        wu = lax.dot_general(lhs[u], rhs, (((1,), (1,)), ((), ())), preferred_element_type=F32)
        w[u] = jnp.where(mask, wu, 0.0)
    top = {u: bf(w[u][:c]) for u in units}
    bot = {u: bf(w[u][c:]) for u in units}
    ap = {u: w[u][:c, :c] for u in units}
    inv = {u: eye + ap[u] for u in units}
    for _ in range(int(math.log2(c)) - 1):
        apb = {u: bf(ap[u]) for u in units}
        ap = {u: dot(apb[u], apb[u]) for u in units}
        invb = {u: bf(inv[u]) for u in units}
        inv = {u: inv[u] + dot(invb[u], bf(ap[u])) for u in units}
    invb = {u: bf(inv[u]) for u in units}

    for ci in chunks:
        us = [(ci, h) for h in heads]
        h0 = [state_ref[h] for h in heads]
        x0 = [dot(lhs[u], bf(h0[u[1]])) for u in us]
        v = [blk(v_ref, u) for u in us]
        sys_rhs = [x0[h][:c] + dot(top[(ci, h)], jnp.concatenate([zeros_cn, v[h]], axis=0)) for h in heads]
        uv = [jnp.concatenate([bf(dot(invb[(ci, h)], bf(sys_rhs[h]))), v[h]], axis=0) for h in heads]
        for h in heads:
            rs = slice(ci * c, (ci + 1) * c)
            cs = slice(h * n, (h + 1) * n)
            yn_ref[rs, cs] = x0[h][c:] + dot(bot[(ci, h)], uv[h])
            bkt = jnp.concatenate([bht_ref[cs, rs], kht_ref[cs, rs]], axis=1)
            state_ref[h] = gct_ref[cs, rs] * h0[h] + dot(bkt, uv[h])

    y = yn_ref[...]
    inv_n = 1.0 / n
    d = y - _group_sum(y, n) * inv_n
    var = _group_sum(d * d, n) * inv_n
    yn = d * lax.rsqrt(var + RWKV_LN_EPS)
    y_ref[...] = ((yn * lnw_ref[...] + lnb_ref[...] + bonus_ref[...]) * g_ref[...]).astype(BF16)


def _rwkv_chunks(at, rt, bt, kt, v, bht, kht, gct, g, bonus, lnw, lnb):
    tp, w = at.shape
    rows = RWKV_ROWS
    tok = pl.BlockSpec((rows, w), lambda i: (i, 0))
    chan = pl.BlockSpec((w, rows), lambda i: (0, i))
    vec = pl.BlockSpec((1, w), lambda i: (0, 0))
    return pl.pallas_call(
        _rwkv_chunk_body,
        grid=(tp // rows,),
        in_specs=[tok, tok, tok, tok, tok, chan, chan, chan, tok, tok, vec, vec],
        out_specs=tok,
        out_shape=jax.ShapeDtypeStruct((tp, w), BF16),
        scratch_shapes=[pltpu.VMEM((RWKV_HEADS, RWKV_HEAD_DIM, RWKV_HEAD_DIM), F32),
                        pltpu.VMEM((rows, w), F32)],
        compiler_params=_cparams(("arbitrary",)),
        name="rwkv_chunks",
    )(at, rt, bt, kt, v, bht, kht, gct, g, bonus, lnw, lnb)


def _attn_pre_body(q_ref, k_ref, v_ref, qg_ref, kg_ref, qz_ref, kn_ref, vt_ref):
    d = DIFF_QK_DIM

    def qk_norm(x, gain):
        ms = _group_sum(x * x, d) * (1.0 / d)
        return x * lax.rsqrt(ms + NORM_EPS) * gain

    q = qk_norm(q_ref[...], qg_ref[...]) * (d ** -0.5 * math.log2(math.e))
    first = (lax.broadcasted_iota(jnp.int32, q.shape, 1) % (2 * d)) < d
    qz_ref[0] = jnp.where(first, q, 0.0).astype(BF16)
    qz_ref[1] = jnp.where(first, 0.0, q).astype(BF16)
    kn_ref[...] = qk_norm(k_ref[...], kg_ref[...]).astype(BF16)
    vt_ref[...] = v_ref[...].T.astype(BF16)


def _attn_pre(p, q_gain, k_gain):
    tp = p.shape[0]
    w = DIFF_WIDTH
    vec = pl.BlockSpec((1, w), lambda i: (0, 0))
    blk = lambda j: pl.BlockSpec((ROW_TILE, w), lambda i: (i, j))
    return pl.pallas_call(
        _attn_pre_body,
        grid=(tp // ROW_TILE,),
        in_specs=[blk(0), blk(1), blk(2), vec, vec],
        out_specs=[pl.BlockSpec((2, ROW_TILE, w), lambda i: (0, i, 0)), blk(0),
                   pl.BlockSpec((w, ROW_TILE), lambda i: (0, i))],
        out_shape=[jax.ShapeDtypeStruct((2, tp, w), BF16), jax.ShapeDtypeStruct((tp, w), BF16),
                   jax.ShapeDtypeStruct((w, tp), BF16)],
        compiler_params=_cparams(("arbitrary",)),
        name="attn_pre",
    )(p, p, p, q_gain, k_gain)


def _flash_body(qi_ref, ki_ref, qz_ref, k_ref, vt_ref, lam_ref, sw_ref, o_ref, m_ref, l_ref, acc_ref,
                *, first_key, lambda_init):
    step = pl.program_id(1)
    qi = qi_ref[step]
    ki = ki_ref[step]
    tq = o_ref.shape[0]
    tk = k_ref.shape[0]

    @pl.when(ki == 0)
    def _():
        m_ref[...] = jnp.full_like(m_ref, -jnp.inf)
        l_ref[...] = jnp.zeros_like(l_ref)
        acc_ref[...] = jnp.zeros_like(acc_ref)

    q = qz_ref[...].reshape(2 * tq, qz_ref.shape[2])
    s = lax.dot_general(k_ref[...], q, (((1,), (1,)), ((), ())), preferred_element_type=F32)

    def update(s):
        m_prev = m_ref[...]
        m_new = jnp.maximum(m_prev, jnp.max(s, axis=0, keepdims=True))
        alpha = jnp.exp2(m_prev - m_new)
        p = jnp.exp2(s - m_new)
        l_ref[...] = alpha * l_ref[...] + jnp.sum(p, axis=0, keepdims=True)
        acc_ref[...] = alpha * acc_ref[...] + jnp.dot(vt_ref[...], p.astype(BF16),
                                                       preferred_element_type=F32)
        m_ref[...] = m_new

    needs_mask = (ki == qi) | (ki == 0)

    @pl.when(needs_mask)
    def _():
        k_pos = ki * tk + lax.broadcasted_iota(jnp.int32, (tk, 2 * tq), 0)
        q_pos = qi * tq + lax.broadcasted_iota(jnp.int32, (tk, 2 * tq), 1) % tq
        visible = (k_pos <= q_pos) & ((k_pos >= first_key) | (q_pos < first_key))
        update(jnp.where(visible, s, -jnp.inf))

    @pl.when(jnp.logical_not(needs_mask))
    def _():
        update(s)

    @pl.when(ki == qi)
    def _():
        lp = lam_ref[...]
        lam = (jnp.exp(jnp.sum(lp[0:1] * lp[1:2])) - jnp.exp(jnp.sum(lp[2:3] * lp[3:4])) + lambda_init)
        o = acc_ref[...] / l_ref[...]
        o = o[:, :tq] - lam * o[:, tq:]
        ms = jnp.mean(o * o, axis=0, keepdims=True)
        o = (o * lax.rsqrt(ms + SUBLN_EPS)).T * (sw_ref[...] * (1.0 - lambda_init))
        o_ref[...] = o.astype(BF16)


def _flash(qz, kn, vt, lam_p, subln_w, first_key, lambda_init):
    _, tp, w = qz.shape
    e = DIFF_V_DIM
    nb = tp // ROW_TILE
    pairs = [(qi, ki) for qi in range(nb) for ki in range(qi + 1)]
    qi_tab = jnp.asarray(np.array([p[0] for p in pairs], np.int32))
    ki_tab = jnp.asarray(np.array([p[1] for p in pairs], np.int32))
    grid_spec = pltpu.PrefetchScalarGridSpec(
        num_scalar_prefetch=2,
        grid=(DIFF_HEADS, len(pairs)),
        in_specs=[
            pl.BlockSpec((2, ROW_TILE, e), lambda h, s, qt, kt: (0, qt[s], h)),
            pl.BlockSpec((ROW_TILE, e), lambda h, s, qt, kt: (kt[s], h)),
            pl.BlockSpec((e, ROW_TILE), lambda h, s, qt, kt: (h, kt[s])),
            pl.BlockSpec(lam_p.shape, lambda h, s, qt, kt: (0, 0)),
            pl.BlockSpec((1, e), lambda h, s, qt, kt: (0, 0)),
        ],
        out_specs=pl.BlockSpec((ROW_TILE, e), lambda h, s, qt, kt: (qt[s], h)),
        scratch_shapes=[pltpu.VMEM((1, 2 * ROW_TILE), F32), pltpu.VMEM((1, 2 * ROW_TILE), F32),
                        pltpu.VMEM((e, 2 * ROW_TILE), F32)],
    )
    return pl.pallas_call(
        functools.partial(_flash_body, first_key=first_key, lambda_init=lambda_init),
        grid_spec=grid_spec,
        out_shape=jax.ShapeDtypeStruct((tp, w), BF16),
        compiler_params=_cparams(("arbitrary", "arbitrary")),
        name="diff_flash",
    )(qi_tab, ki_tab, qz, kn, vt, lam_p, subln_w)


def _out_proj_body(ya_ref, yb_ref, wa_ref, wb_ref, h_ref, g_ref, h2_ref, hn_ref):
    acc = jnp.dot(ya_ref[...], wa_ref[...], preferred_element_type=F32)
    acc = acc + jnp.dot(yb_ref[...], wb_ref[...], preferred_element_type=F32)
    h2 = h_ref[...] + acc
    h2_ref[...] = h2
    ms = jnp.mean(h2 * h2, axis=-1, keepdims=True)
    hn_ref[...] = (h2 * lax.rsqrt(ms + NORM_EPS) * g_ref[...]).astype(BF16)


def _out_proj(ya, yb, wa, wb, h, gain, row_tile):
    tp, d = h.shape
    rows = pl.BlockSpec((row_tile, d), lambda i: (i, 0))
    return pl.pallas_call(
        _out_proj_body,
        grid=(tp // row_tile,),
        in_specs=[
            pl.BlockSpec((row_tile, ya.shape[1]), lambda i: (i, 0)),
            pl.BlockSpec((row_tile, yb.shape[1]), lambda i: (i, 0)),
            pl.BlockSpec(wa.shape, lambda i: (0, 0)),
            pl.BlockSpec(wb.shape, lambda i: (0, 0)),
            rows,
            pl.BlockSpec((1, d), lambda i: (0, 0)),
        ],
        out_specs=[rows, rows],
        out_shape=[jax.ShapeDtypeStruct((tp, d), F32), jax.ShapeDtypeStruct((tp, d), BF16)],
        compiler_params=_cparams(("arbitrary",)),
        name="out_proj",
    )(ya, yb, wa, wb, h, gain)


def _ffn_body(x_ref, halo_ref, wg_ref, wu_ref, cwg_ref, cwu_ref, cbg_ref, cbu_ref, wd_ref, h2_ref,
              o_ref, acc_ref):
    j = pl.program_id(1)
    rows = x_ref.shape[0]
    hrows = halo_ref.shape[0]
    row = lax.broadcasted_iota(jnp.int32, (rows, 1), 0)

    def conv(w_ref, cw_ref, cb_ref):
        u = jnp.dot(x_ref[...], w_ref[...], preferred_element_type=F32)
        uh = jnp.dot(halo_ref[...], w_ref[...], preferred_element_type=F32)
        p1, p2 = uh[hrows - 1:hrows], uh[hrows - 2:hrows - 1]
        u1 = jnp.where(row == 0, p1, pltpu.roll(u, 1, axis=0))
        u2 = jnp.where(row == 0, p2, jnp.where(row == 1, p1, pltpu.roll(u, 2, axis=0)))
        cw = cw_ref[...]
        return u2 * cw[0:1] + u1 * cw[1:2] + u * cw[2:3] + cb_ref[...]

    gate = conv(wg_ref, cwg_ref, cbg_ref)
    up = conv(wu_ref, cwu_ref, cbu_ref)
    act = (gate * _sigmoid(gate) * up).astype(BF16)
    part = jnp.dot(act, wd_ref[...], preferred_element_type=F32)

    @pl.when(j == 0)
    def _():
        acc_ref[...] = part

    @pl.when(j > 0)
    def _():
        acc_ref[...] += part

    @pl.when(j == pl.num_programs(1) - 1)
    def _():
        o_ref[...] = h2_ref[...] + acc_ref[...]


def _ffn(hn, h2, w_up, conv_w, conv_b, w_down, first_block, n_out_rows, ff_tile):
    tp, d = hn.shape
    dff = w_down.shape[0]
    nf = dff // ff_tile
    halo = 16
    per = ROW_TILE // halo
    fb = first_block
    return pl.pallas_call(
        _ffn_body,
        grid=(n_out_rows // ROW_TILE, nf),
        in_specs=[
            pl.BlockSpec((ROW_TILE, d), lambda i, j: (i + fb, 0)),
            pl.BlockSpec((halo, d), lambda i, j: ((i + fb) * per - 1, 0)),
            pl.BlockSpec((d, ff_tile), lambda i, j: (0, j)),
            pl.BlockSpec((d, ff_tile), lambda i, j: (0, j + nf)),
            pl.BlockSpec((CONV_WIDTH, ff_tile), lambda i, j: (0, j)),
            pl.BlockSpec((CONV_WIDTH, ff_tile), lambda i, j: (0, j + nf)),
            pl.BlockSpec((1, ff_tile), lambda i, j: (0, j)),
            pl.BlockSpec((1, ff_tile), lambda i, j: (0, j + nf)),
            pl.BlockSpec((ff_tile, d), lambda i, j: (j, 0)),
            pl.BlockSpec((ROW_TILE, d), lambda i, j: (i + fb, 0)),
        ],
        out_specs=pl.BlockSpec((ROW_TILE, d), lambda i, j: (i, 0)),
        out_shape=jax.ShapeDtypeStruct((n_out_rows, d), F32),
        scratch_shapes=[pltpu.VMEM((ROW_TILE, d), F32)],
        compiler_params=_cparams(("arbitrary", "arbitrary")),
        name="conv_glu_ffn",
    )(hn, hn, w_up, w_up, conv_w, conv_w, conv_b, conv_b, w_down, h2)


def _pad_cols(a, width):
    return jnp.pad(a, ((0, 0), (0, width - a.shape[1])))


def _pad_rows(a, height):
    return jnp.pad(a, ((0, height - a.shape[0]), (0, 0)))


def _layer(h, first_key, lambda_init, norm1_g, w_in, mu, w0, w2, a0, a2, g2, k_k, k_a, r_k, lnx_w, lnx_b,
           q_gain, k_gain, lam_q1, lam_k1, lam_q2, lam_k2, subln_w, w_out, norm2_g, w_up, conv_w, conv_b,
           w_down):
    w = RWKV_WIDTH
    row = lambda a: a.reshape(1, -1)
    rwkv_cols = 3 * w + DECAY_LORA + AAA_LORA + GATE_LORA
    cuts = np.cumsum([3 * w, DECAY_LORA, AAA_LORA])
    wi_rkv, wi_wd, wi_ad, wi_gd = jnp.split(w_in[:, :rwkv_cols], cuts, axis=1)
    mu_rkv, mu_wd, mu_ad, mu_gd = jnp.split(row(mu), cuts, axis=1)
    pads = LORA_PAD
    w_proj = jnp.concatenate(
        [w_in[:, rwkv_cols:], wi_rkv, _pad_cols(wi_wd, pads[0]), _pad_cols(wi_ad, pads[1]),
         _pad_cols(wi_gd, pads[2])], axis=1).astype(BF16)
    mu_l = jnp.concatenate([_pad_cols(mu_wd, pads[0]), _pad_cols(mu_ad, pads[1]), _pad_cols(mu_gd, pads[2])],
                           axis=1)

    p = _proj(h, row(norm1_g), w_proj, PROJ_COLS // 4)

    at, rt, bt, kt, v, bht, kht, gct, g, bonus = _rwkv_pre(
        p, mu_rkv, mu_l, row(w0), _pad_rows(w2, pads[0]), row(a0), _pad_rows(a2, pads[1]),
        _pad_rows(g2, pads[2]), row(k_k), row(k_a), row(r_k))
    y_a = _rwkv_chunks(at, rt, bt, kt, v, bht, kht, gct, g, bonus, row(lnx_w), row(lnx_b))

    tile_heads = lambda a: jnp.tile(row(a), (1, 2 * DIFF_HEADS))
    qz, kn, vt = _attn_pre(p, tile_heads(q_gain), tile_heads(k_gain))
    lam_p = jnp.stack([lam_q1, lam_k1, lam_q2, lam_k2]).astype(F32)
    y_b = _flash(qz, kn, vt, lam_p, row(subln_w), first_key, lambda_init)

    w_out_b = w_out.astype(BF16)
    h2, hn2 = _out_proj(y_a, y_b, w_out_b[:w], w_out_b[w:], h, row(norm2_g), ROW_TILE // 2)

    n_x = h.shape[0] - ROW_TILE
    return _ffn(hn2, h2, w_up.astype(BF16), conv_w, row(conv_b), w_down.astype(BF16), 1, n_x, 512)


def kernel(x, meta, norm1_g, w_in, mu, w0, w2, a0, a2, g2, k_k, k_a, r_k, lnx_w, lnx_b, q_gain, k_gain,
           lam_q1, lam_k1, lam_q2, lam_k2, subln_w, w_out, norm2_g, w_up, conv_w, conv_b, w_down):
    batch, seq, d = x.shape
    depth = norm1_g.shape[0]
    assert depth == 1 and seq % ROW_TILE == 0 and meta.shape[0] == N_META
    front = ROW_TILE - N_META
    outs = []
    for bi in range(batch):
        h = jnp.concatenate([jnp.zeros((front, d), x.dtype), meta.astype(x.dtype), x[bi]], axis=0)
        lambda_init = 0.8 - 0.6 * math.exp(-0.3 * 0)
        outs.append(_layer(
            h, front, lambda_init, norm1_g[0], w_in[0], mu[0], w0[0], w2[0], a0[0], a2[0], g2[0], k_k[0],
            k_a[0], r_k[0], lnx_w[0], lnx_b[0], q_gain[0], k_gain[0], lam_q1[0], lam_k1[0], lam_q2[0],
            lam_k2[0], subln_w[0], w_out[0], norm2_g[0], w_up[0], conv_w[0], conv_b[0], w_down[0]))
    return jnp.stack(outs, axis=0)
```

```python
import functools
import math

import jax
import jax.numpy as jnp
import numpy as np
from jax import lax
from jax.experimental import pallas as pl
from jax.experimental.pallas import tpu as pltpu

F32 = jnp.float32
BF16 = jnp.bfloat16

N_META = 16
RWKV_HEADS = 16
RWKV_HEAD_DIM = 64
RWKV_WIDTH = RWKV_HEADS * RWKV_HEAD_DIM
DECAY_LORA = 64
AAA_LORA = 64
GATE_LORA = 160
RWKV_LN_EPS = 64e-5
DIFF_HEADS = 8
DIFF_QK_DIM = 64
DIFF_V_DIM = 128
DIFF_WIDTH = DIFF_HEADS * DIFF_V_DIM
CONV_WIDTH = 3
NORM_EPS = 1e-6
SUBLN_EPS = 1e-5

LANES = 128
ROW_TILE = 512
CHUNK = 64
RWKV_ROWS = 2 * CHUNK
LORA_PAD = (128, 128, 256)
LORA_COLS = sum(LORA_PAD)
PROJ_COLS = 3 * DIFF_WIDTH + 3 * RWKV_WIDTH + LORA_COLS
VMEM_LIMIT = 56 * 1024 * 1024


def _cparams(sem):
    return pltpu.CompilerParams(dimension_semantics=sem, vmem_limit_bytes=VMEM_LIMIT)


def _split_dot(a, b_bf16, parts):
    acc = None
    rem = a
    for _ in range(parts):
        hi = rem.astype(BF16)
        term = jnp.dot(hi, b_bf16, preferred_element_type=F32)
        acc = term if acc is None else acc + term
        rem = rem - hi.astype(F32)
    return acc


def _dot3(a, b):
    ah = a.astype(BF16)
    al = (a - ah.astype(F32)).astype(BF16)
    bh = b.astype(BF16)
    bl = (b - bh.astype(F32)).astype(BF16)
    d = functools.partial(jnp.dot, preferred_element_type=F32)
    return d(ah, bh) + (d(ah, bl) + d(al, bh))


def _group_ones(n, group):
    r = lax.broadcasted_iota(jnp.int32, (n, n), 0) // group
    c = lax.broadcasted_iota(jnp.int32, (n, n), 1) // group
    return (r == c).astype(BF16)


def _group_sum(x, group):
    ones = _group_ones(LANES, group)
    cols = [
        _split_dot(x[:, c:c + LANES], ones, 2)
        for c in range(0, x.shape[1], LANES)
    ]
    return jnp.concatenate(cols, axis=1)


def _proj_body(head_ref, x_ref, g_ref, w_ref, o_ref, hn_ref):
    @pl.when(pl.program_id(1) == 0)
    def _():
        x = jnp.where(pl.program_id(0) == 0, head_ref[...], x_ref[...])
        ms = jnp.mean(x * x, axis=-1, keepdims=True)
        hn_ref[...] = (x * lax.rsqrt(ms + NORM_EPS) * g_ref[...]).astype(BF16)

    o_ref[...] = jnp.dot(hn_ref[...], w_ref[...], preferred_element_type=F32)


def _proj(head, x, gain, w, col_tile):
    d = x.shape[1]
    tp = head.shape[0] + x.shape[0]
    n = w.shape[1]
    return pl.pallas_call(
        _proj_body,
        grid=(tp // ROW_TILE, n // col_tile),
        in_specs=[
            pl.BlockSpec((ROW_TILE, d), lambda i, j: (0, 0)),
            pl.BlockSpec((ROW_TILE, d), lambda i, j: (jnp.maximum(i - 1, 0), 0)),
            pl.BlockSpec((1, d), lambda i, j: (0, 0)),
            pl.BlockSpec((d, col_tile), lambda i, j: (0, j)),
        ],
        out_specs=pl.BlockSpec((ROW_TILE, col_tile), lambda i, j: (i, j)),
        out_shape=jax.ShapeDtypeStruct((tp, n), F32),
        scratch_shapes=[pltpu.VMEM((ROW_TILE, d), BF16)],
        compiler_params=_cparams(("arbitrary", "arbitrary")),
        name="proj",
    )(head, x, gain, w)


def _softplus(z):
    return jnp.maximum(z, 0.0) + jnp.log(1.0 + jnp.exp(-jnp.abs(z)))


def _sigmoid(z):
    return 1.0 / (1.0 + jnp.exp(-z))


def _rwkv_pre_body(r_ref, k_ref, v_ref, l_ref, mu_ref, mul_ref, w0_ref, w2_ref, a0_ref, a2_ref,
                   g2_ref, kk_ref, ka_ref, rk_ref,
                   at_ref, rt_ref, bt_ref, kt_ref, vo_ref, bht_ref, kht_ref, gct_ref, g_ref, bonus_ref,
                   last_ref, lastl_ref):
    i = pl.program_id(0)

    @pl.when(i == 0)
    def _():
        last_ref[...] = jnp.zeros_like(last_ref)
        lastl_ref[...] = jnp.zeros_like(lastl_ref)

    rows = r_ref.shape[0]
    row = lax.broadcasted_iota(jnp.int32, (rows, 1), 0)

    def shift_mix(p, last_row, mu):
        prev = jnp.where(row == 0, last_row, pltpu.roll(p, 1, axis=0))
        return p + (prev - p) * mu

    w = RWKV_WIDTH
    r_raw, k_raw, v_raw, l_raw = r_ref[...], k_ref[...], v_ref[...], l_ref[...]
    r = shift_mix(r_raw, last_ref[0:1, 0:w], mu_ref[:, 0:w])
    k = shift_mix(k_raw, last_ref[0:1, w:2 * w], mu_ref[:, w:2 * w])
    v = shift_mix(v_raw, last_ref[0:1, 2 * w:3 * w], mu_ref[:, 2 * w:3 * w])
    lo = shift_mix(l_raw, lastl_ref[0:1, :], mul_ref[...])
    last_ref[0:1, 0:w] = r_raw[rows - 1:rows]
    last_ref[0:1, w:2 * w] = k_raw[rows - 1:rows]
    last_ref[0:1, 2 * w:3 * w] = v_raw[rows - 1:rows]
    lastl_ref[0:1, :] = l_raw[rows - 1:rows]

    c0, c1 = LORA_PAD[0], LORA_PAD[0] + LORA_PAD[1]
    wd, ad, gd = lo[:, 0:c0], lo[:, c0:c1], lo[:, c1:]
    wlog = -_softplus(-(w0_ref[...] + _dot3(jnp.tanh(wd), w2_ref[...]))) - 0.5
    logd = -jnp.exp(wlog)
    a = _sigmoid(a0_ref[...] + _dot3(ad, a2_ref[...]))
    g_ref[...] = _dot3(_sigmoid(gd), g2_ref[...])

    kk = k * kk_ref[...]
    kk = kk / jnp.maximum(jnp.sqrt(_group_sum(kk * kk, RWKV_HEAD_DIM)), 1e-12)
    k = k * (1.0 + (a - 1.0) * ka_ref[...])
    b = kk * a
    bonus_ref[...] = _group_sum(r * k * rk_ref[...], RWKV_HEAD_DIM) * v

    tr = lax.broadcasted_iota(jnp.int32, (rows, rows), 0)
    tc = lax.broadcasted_iota(jnp.int32, (rows, rows), 1)
    same = (tr // CHUNK) == (tc // CHUNK)
    cum = _split_dot_lhs((same & (tc <= tr)).astype(BF16), logd)
    tot = _split_dot_lhs(same.astype(BF16), logd)

    e_prev = jnp.exp(cum - logd)
    e_cum = jnp.exp(cum)
    e_neg = jnp.exp(-cum)
    e_rem = jnp.exp(tot - cum)
    at_ref[...] = (-kk * e_prev).astype(BF16)
    rt_ref[...] = (r * e_cum).astype(BF16)
    bt_ref[...] = (b * e_neg).astype(BF16)
    kt_ref[...] = (k * e_neg).astype(BF16)
    vo_ref[...] = v.astype(BF16)
    bht_ref[...] = (b * e_rem).T.astype(BF16)
    kht_ref[...] = (k * e_rem).T.astype(BF16)
    gct_ref[...] = jnp.exp(tot).T


def _split_dot_lhs(a_bf16, b):
    acc = None
    rem = b
    for _ in range(3):
        hi = rem.astype(BF16)
        term = jnp.dot(a_bf16, hi, preferred_element_type=F32)
        acc = term if acc is None else acc + term
        rem = rem - hi.astype(F32)
    return acc


def _rwkv_pre(p, mu, mul, w0, w2, a0, a2, g2, k_k, k_a, r_k):
    tp = p.shape[0]
    w = RWKV_WIDTH
    rows = RWKV_ROWS
    nqkv = 3 * DIFF_WIDTH // w
    lora_blk = (3 * DIFF_WIDTH + 3 * w) // LORA_COLS

    def full(arr):
        return pl.BlockSpec(arr.shape, lambda i: (0, 0))

    tok = pl.BlockSpec((rows, w), lambda i: (i, 0))
    chan = pl.BlockSpec((w, rows), lambda i: (0, i))
    tok_bf = jax.ShapeDtypeStruct((tp, w), BF16)
    return pl.pallas_call(
        _rwkv_pre_body,
        grid=(tp // rows,),
        in_specs=[
            pl.BlockSpec((rows, w), lambda i: (i, nqkv)),
            pl.BlockSpec((rows, w), lambda i: (i, nqkv + 1)),
            pl.BlockSpec((rows, w), lambda i: (i, nqkv + 2)),
            pl.BlockSpec((rows, LORA_COLS), lambda i: (i, lora_blk)),
            full(mu), full(mul), full(w0), full(w2), full(a0), full(a2), full(g2),
            full(k_k), full(k_a), full(r_k),
        ],
        out_specs=[tok, tok, tok, tok, tok, chan, chan, chan, tok, tok],
        out_shape=[tok_bf, tok_bf, tok_bf, tok_bf, tok_bf,
                   jax.ShapeDtypeStruct((w, tp), BF16), jax.ShapeDtypeStruct((w, tp), BF16),
                   jax.ShapeDtypeStruct((w, tp), F32),
                   jax.ShapeDtypeStruct((tp, w), F32), jax.ShapeDtypeStruct((tp, w), F32)],
        scratch_shapes=[pltpu.VMEM((8, 3 * w), F32), pltpu.VMEM((8, LORA_COLS), F32)],
        compiler_params=_cparams(("arbitrary",)),
        name="rwkv_pre",
    )(p, p, p, p, mu, mul, w0, w2, a0, a2, g2, k_k, k_a, r_k)


def _rwkv_chunk_body(at_ref, rt_ref, bt_ref, kt_ref, v_ref, bht_ref, kht_ref, gct_ref, g_ref, bonus_ref,
                     lnw_ref, lnb_ref, y_ref, state_ref, yn_ref):
    n = RWKV_HEAD_DIM
    c = CHUNK

    @pl.when(pl.program_id(0) == 0)
    def _():
        state_ref[...] = jnp.zeros_like(state_ref)

    tr = lax.broadcasted_iota(jnp.int32, (2 * c, 2 * c), 0)
    tc = lax.broadcasted_iota(jnp.int32, (2 * c, 2 * c), 1) % c
    mask = ((tr < c) & (tc < tr)) | ((tr >= c) & (tc <= tr - c))
    er = lax.broadcasted_iota(jnp.int32, (c, 2 * c), 0)
    ec = lax.broadcasted_iota(jnp.int32, (c, 2 * c), 1)
    right = ec >= c
    eye2 = (ec == er + c).astype(F32)
    zeros_cn = jnp.zeros((c, n), BF16)
    dot = functools.partial(jnp.dot, preferred_element_type=F32)
    bf = lambda t: t.astype(BF16)

    heads = range(RWKV_HEADS)
    chunks = range(at_ref.shape[0] // c)
    units = [(ci, h) for ci in chunks for h in heads]

    def blk(ref, u):
        return ref[u[0] * c:(u[0] + 1) * c, u[1] * n:(u[1] + 1) * n]

    lhs = {u: jnp.concatenate([blk(at_ref, u), blk(rt_ref, u)], axis=0) for u in units}
    w = {}
    for u in units:
        rhs = jnp.concatenate([blk(bt_ref, u), blk(kt_ref, u)], axis=0)
        wu = lax.dot_general(lhs[u], rhs, (((1,), (1,)), ((), ())), preferred_element_type=F32)
        w[u] = jnp.where(mask, wu, 0.0)
    top = {u: bf(w[u][:c]) for u in units}
    bot = {u: bf(w[u][c:]) for u in units}
    xs = {u: jnp.where(right, eye2, w[u][:c]) for u in units}
    for _ in range(int(math.log2(c))):
        xb = {u: bf(xs[u]) for u in units}
        xs = {u: dot(xb[u][:, :c], xb[u]) + jnp.where(right, xs[u], 0.0) for u in units}
    invb = {u: bf(xs[u]) for u in units}

    for ci in chunks:
        us = [(ci, h) for h in heads]
        h0 = [state_ref[h] for h in heads]
        x0 = [dot(lhs[u], bf(h0[u[1]])) for u in us]
        v = [blk(v_ref, u) for u in us]
        sys_rhs = [x0[h][:c] + dot(top[(ci, h)], jnp.concatenate([zeros_cn, v[h]], axis=0)) for h in heads]
        u_sol = [dot(invb[(ci, h)], jnp.concatenate([zeros_cn, bf(sys_rhs[h])], axis=0)) for h in heads]
        uv = [jnp.concatenate([bf(u_sol[h]), v[h]], axis=0) for h in heads]
        for h in heads:
            rs = slice(ci * c, (ci + 1) * c)
            cs = slice(h * n, (h + 1) * n)
            yn_ref[rs, cs] = x0[h][c:] + dot(bot[(ci, h)], uv[h])
            bkt = jnp.concatenate([bht_ref[cs, rs], kht_ref[cs, rs]], axis=1)
            state_ref[h] = gct_ref[cs, rs] * h0[h] + dot(bkt, uv[h])

    y = yn_ref[...]
    inv_n = 1.0 / n
    d = y - _group_sum(y, n) * inv_n
    var = _group_sum(d * d, n) * inv_n
    yn = d * lax.rsqrt(var + RWKV_LN_EPS)
    y_ref[...] = ((yn * lnw_ref[...] + lnb_ref[...] + bonus_ref[...]) * g_ref[...]).astype(BF16)


def _rwkv_chunks(at, rt, bt, kt, v, bht, kht, gct, g, bonus, lnw, lnb):
    tp, w = at.shape
    rows = RWKV_ROWS
    tok = pl.BlockSpec((rows, w), lambda i: (i, 0))
    chan = pl.BlockSpec((w, rows), lambda i: (0, i))
    vec = pl.BlockSpec((1, w), lambda i: (0, 0))
    return pl.pallas_call(
        _rwkv_chunk_body,
        grid=(tp // rows,),
        in_specs=[tok, tok, tok, tok, tok, chan, chan, chan, tok, tok, vec, vec],
        out_specs=tok,
        out_shape=jax.ShapeDtypeStruct((tp, w), BF16),
        scratch_shapes=[pltpu.VMEM((RWKV_HEADS, RWKV_HEAD_DIM, RWKV_HEAD_DIM), F32),
                        pltpu.VMEM((rows, w), F32)],
        compiler_params=_cparams(("arbitrary",)),
        name="rwkv_chunks",
    )(at, rt, bt, kt, v, bht, kht, gct, g, bonus, lnw, lnb)


def _attn_pre_body(q_ref, k_ref, v_ref, qg_ref, kg_ref, qz_ref, kn_ref, vt_ref):
    d = DIFF_QK_DIM

    def qk_norm(x, gain):
        ms = _group_sum(x * x, d) * (1.0 / d)
        return x * lax.rsqrt(ms + NORM_EPS) * gain

    q = qk_norm(q_ref[...], qg_ref[...]) * (d ** -0.5 * math.log2(math.e))
    first = (lax.broadcasted_iota(jnp.int32, q.shape, 1) % (2 * d)) < d
    qz_ref[0] = jnp.where(first, q, 0.0).astype(BF16)
    qz_ref[1] = jnp.where(first, 0.0, q).astype(BF16)
    kn_ref[...] = qk_norm(k_ref[...], kg_ref[...]).astype(BF16)
    vt_ref[...] = v_ref[...].T.astype(BF16)


def _attn_pre(p, q_gain, k_gain):
    tp = p.shape[0]
    w = DIFF_WIDTH
    vec = pl.BlockSpec((1, w), lambda i: (0, 0))
    blk = lambda j: pl.BlockSpec((ROW_TILE, w), lambda i: (i, j))
    return pl.pallas_call(
        _attn_pre_body,
        grid=(tp // ROW_TILE,),
        in_specs=[blk(0), blk(1), blk(2), vec, vec],
        out_specs=[pl.BlockSpec((2, ROW_TILE, w), lambda i: (0, i, 0)), blk(0),
                   pl.BlockSpec((w, ROW_TILE), lambda i: (0, i))],
        out_shape=[jax.ShapeDtypeStruct((2, tp, w), BF16), jax.ShapeDtypeStruct((tp, w), BF16),
                   jax.ShapeDtypeStruct((w, tp), BF16)],
        compiler_params=_cparams(("arbitrary",)),
        name="attn_pre",
    )(p, p, p, q_gain, k_gain)


FLASH_HEADS = 4
FLASH_QCOLS = 256


def _flash_body(qi_ref, ki_ref, qz_ref, k_ref, vt_ref, lam_ref, sw_ref, o_ref, m_ref, l_ref, acc_ref,
                *, first_key, lambda_init):
    step = pl.program_id(1)
    qi = qi_ref[step]
    ki = ki_ref[step]
    tq = o_ref.shape[0]
    tk = k_ref.shape[0]
    e = DIFF_V_DIM
    heads = range(FLASH_HEADS)
    lanes = [slice(g * e, (g + 1) * e) for g in heads]

    @pl.when(ki == 0)
    def _():
        m_ref[...] = jnp.full_like(m_ref, -jnp.inf)
        l_ref[...] = jnp.zeros_like(l_ref)
        acc_ref[...] = jnp.zeros_like(acc_ref)

    def strand(g, c0, rows, masked):
        cols = slice(c0, c0 + FLASH_QCOLS)
        which, off = divmod(c0, tq)
        n = rows.stop - rows.start
        st = {}

        def stage_scores():
            q = qz_ref[which, off:off + FLASH_QCOLS, lanes[g]]
            s = lax.dot_general(k_ref[rows, lanes[g]], q, (((1,), (1,)), ((), ())),
                                preferred_element_type=F32)
            if masked:
                k_pos = ki * tk + rows.start + lax.broadcasted_iota(jnp.int32, (n, FLASH_QCOLS), 0)
                q_pos = qi * tq + off + lax.broadcasted_iota(jnp.int32, (n, FLASH_QCOLS), 1)
                vis = (k_pos <= q_pos) & ((k_pos >= first_key) | (q_pos < first_key))
                s = jnp.where(vis, s, -jnp.inf)
            st["s"] = s

        def stage_softmax():
            m_prev = m_ref[g, :, cols]
            m_new = jnp.maximum(m_prev, jnp.max(st["s"], axis=0, keepdims=True))
            st["alpha"] = jnp.exp2(m_prev - m_new)
            p = jnp.exp2(st["s"] - m_new)
            l_ref[g, :, cols] = st["alpha"] * l_ref[g, :, cols] + jnp.sum(p, axis=0, keepdims=True)
            m_ref[g, :, cols] = m_new
            st["p"] = p.astype(BF16)

        def stage_values():
            acc_ref[g, :, cols] = st["alpha"] * acc_ref[g, :, cols] + jnp.dot(
                vt_ref[lanes[g], rows], st["p"], preferred_element_type=F32)

        return [stage_scores, stage_softmax, stage_values]

    def run(rows_for, masked):
        strands = [strand(g, c0, rows_for(c0 % tq), masked)
                   for c0 in range(0, 2 * tq, FLASH_QCOLS) for g in heads]
        depth = len(strands[0])
        for t in range(len(strands) + depth - 1):
            for stage in range(depth):
                if 0 <= t - stage < len(strands):
                    strands[t - stage][stage]()

    everything = slice(0, tk)
    tail = slice(first_key // LANES * LANES, tk)

    @pl.when(ki == qi)
    def _():
        run(lambda off: slice(0, off + FLASH_QCOLS), True)

    @pl.when((ki == 0) & (qi > 0))
    def _():
        run(lambda off: tail, True)

    @pl.when((ki > 0) & (ki < qi))
    def _():
        run(lambda off: everything, False)

    @pl.when(ki == qi)
    def _():
        lp = lam_ref[...]
        lam = (jnp.exp(jnp.sum(lp[0:1] * lp[1:2])) - jnp.exp(jnp.sum(lp[2:3] * lp[3:4])) + lambda_init)
        for g in heads:
            o = acc_ref[g] / l_ref[g]
            o = o[:, :tq] - lam * o[:, tq:]
            ms = jnp.mean(o * o, axis=0, keepdims=True)
            o = (o * lax.rsqrt(ms + SUBLN_EPS)).T * (sw_ref[...] * (1.0 - lambda_init))
            o_ref[:, lanes[g]] = o.astype(BF16)


def _flash(qz, kn, vt, lam_p, subln_w, first_key, lambda_init):
    _, tp, w = qz.shape
    e = DIFF_V_DIM
    we = FLASH_HEADS * e
    nb = tp // ROW_TILE
    pairs = [(qi, ki) for qi in range(nb) for ki in range(qi + 1)]
    qi_tab = jnp.asarray(np.array([p[0] for p in pairs], np.int32))
    ki_tab = jnp.asarray(np.array([p[1] for p in pairs], np.int32))
    grid_spec = pltpu.PrefetchScalarGridSpec(
        num_scalar_prefetch=2,
        grid=(DIFF_HEADS // FLASH_HEADS, len(pairs)),
        in_specs=[
            pl.BlockSpec((2, ROW_TILE, we), lambda h, s, qt, kt: (0, qt[s], h)),
            pl.BlockSpec((ROW_TILE, we), lambda h, s, qt, kt: (kt[s], h)),
            pl.BlockSpec((we, ROW_TILE), lambda h, s, qt, kt: (h, kt[s])),
            pl.BlockSpec(lam_p.shape, lambda h, s, qt, kt: (0, 0)),
            pl.BlockSpec((1, e), lambda h, s, qt, kt: (0, 0)),
        ],
        out_specs=pl.BlockSpec((ROW_TILE, we), lambda h, s, qt, kt: (qt[s], h)),
        scratch_shapes=[pltpu.VMEM((FLASH_HEADS, 1, 2 * ROW_TILE), F32),
                        pltpu.VMEM((FLASH_HEADS, 1, 2 * ROW_TILE), F32),
                        pltpu.VMEM((FLASH_HEADS, e, 2 * ROW_TILE), F32)],
    )
    return pl.pallas_call(
        functools.partial(_flash_body, first_key=first_key, lambda_init=lambda_init),
        grid_spec=grid_spec,
        out_shape=jax.ShapeDtypeStruct((tp, w), BF16),
        compiler_params=_cparams(("arbitrary", "arbitrary")),
        name="diff_flash",
    )(qi_tab, ki_tab, qz, kn, vt, lam_p, subln_w)


def _out_proj_body(ya_ref, yb_ref, wa_ref, wb_ref, head_ref, x_ref, g_ref, h2_ref, hn_ref, *, head_blocks):
    acc = jnp.dot(ya_ref[...], wa_ref[...], preferred_element_type=F32)
    acc = acc + jnp.dot(yb_ref[...], wb_ref[...], preferred_element_type=F32)
    h2 = jnp.where(pl.program_id(0) < head_blocks, head_ref[...], x_ref[...]) + acc
    h2_ref[...] = h2
    ms = jnp.mean(h2 * h2, axis=-1, keepdims=True)
    hn_ref[...] = (h2 * lax.rsqrt(ms + NORM_EPS) * g_ref[...]).astype(BF16)


def _out_proj(ya, yb, wa, wb, head, x, gain, row_tile):
    d = x.shape[1]
    tp = head.shape[0] + x.shape[0]
    nhb = head.shape[0] // row_tile
    rows = pl.BlockSpec((row_tile, d), lambda i: (i, 0))
    return pl.pallas_call(
        functools.partial(_out_proj_body, head_blocks=nhb),
        grid=(tp // row_tile,),
        in_specs=[
            pl.BlockSpec((row_tile, ya.shape[1]), lambda i: (i, 0)),
            pl.BlockSpec((row_tile, yb.shape[1]), lambda i: (i, 0)),
            pl.BlockSpec(wa.shape, lambda i: (0, 0)),
            pl.BlockSpec(wb.shape, lambda i: (0, 0)),
            pl.BlockSpec((row_tile, d), lambda i: (jnp.minimum(i, nhb - 1), 0)),
            pl.BlockSpec((row_tile, d), lambda i: (jnp.maximum(i - nhb, 0), 0)),
            pl.BlockSpec((1, d), lambda i: (0, 0)),
        ],
        out_specs=[rows, rows],
        out_shape=[jax.ShapeDtypeStruct((tp, d), F32), jax.ShapeDtypeStruct((tp, d), BF16)],
        compiler_params=_cparams(("arbitrary",)),
        name="out_proj",
    )(ya, yb, wa, wb, head, x, gain)


FFN_SUB = 256


def _ffn_body(x_ref, wg_ref, wu_ref, cwg_ref, cwu_ref, cbg_ref, cbu_ref, wd_ref, h2_ref, o_ref, acc_ref):
    j = pl.program_id(1)
    halo = x_ref.shape[0] - o_ref.shape[0]
    x = x_ref[...]
    subs = [slice(s, s + FFN_SUB) for s in range(0, wg_ref.shape[1], FFN_SUB)]

    def conv(u, cw_ref, cb_ref, cols):
        cw = cw_ref[:, cols]
        y = pltpu.roll(u, 2, axis=0) * cw[0:1] + pltpu.roll(u, 1, axis=0) * cw[1:2] + u * cw[2:3]
        return y[halo:] + cb_ref[:, cols]

    ups = [(jnp.dot(x, wg_ref[:, c], preferred_element_type=F32),
            jnp.dot(x, wu_ref[:, c], preferred_element_type=F32)) for c in subs]
    part = None
    for c, (ug, uu) in zip(subs, ups):
        gate = conv(ug, cwg_ref, cbg_ref, c)
        up = conv(uu, cwu_ref, cbu_ref, c)
        act = (gate * _sigmoid(gate) * up).astype(BF16)
        term = jnp.dot(act, wd_ref[c, :], preferred_element_type=F32)
        part = term if part is None else part + term

    @pl.when(j == 0)
    def _():
        acc_ref[...] = part

    @pl.when(j > 0)
    def _():
        acc_ref[...] += part

    @pl.when(j == pl.num_programs(1) - 1)
    def _():
        o_ref[...] = h2_ref[...] + acc_ref[...]


def _ffn(hn, h2, w_up, conv_w, conv_b, w_down, first_block, n_out_rows, ff_tile):
    tp, d = hn.shape
    dff = w_down.shape[0]
    nf = dff // ff_tile
    halo = 16
    fb = first_block
    return pl.pallas_call(
        _ffn_body,
        grid=(n_out_rows // ROW_TILE, nf),
        in_specs=[
            pl.BlockSpec((pl.Element(ROW_TILE + halo), pl.Element(d)),
                         lambda i, j: (pl.multiple_of((i + fb) * ROW_TILE - halo, halo), 0)),
            pl.BlockSpec((d, ff_tile), lambda i, j: (0, j)),
            pl.BlockSpec((d, ff_tile), lambda i, j: (0, j + nf)),
            pl.BlockSpec((CONV_WIDTH, ff_tile), lambda i, j: (0, j)),
            pl.BlockSpec((CONV_WIDTH, ff_tile), lambda i, j: (0, j + nf)),
            pl.BlockSpec((1, ff_tile), lambda i, j: (0, j)),
            pl.BlockSpec((1, ff_tile), lambda i, j: (0, j + nf)),
            pl.BlockSpec((ff_tile, d), lambda i, j: (j, 0)),
            pl.BlockSpec((ROW_TILE, d), lambda i, j: (i + fb, 0)),
        ],
        out_specs=pl.BlockSpec((ROW_TILE, d), lambda i, j: (i, 0)),
        out_shape=jax.ShapeDtypeStruct((n_out_rows, d), F32),
        scratch_shapes=[pltpu.VMEM((ROW_TILE, d), F32)],
        compiler_params=_cparams(("arbitrary", "arbitrary")),
        name="conv_glu_ffn",
    )(hn, w_up, w_up, conv_w, conv_w, conv_b, conv_b, w_down, h2)


def _pad_cols(a, width):
    return jnp.pad(a, ((0, 0), (0, width - a.shape[1])))


def _pad_rows(a, height):
    return jnp.pad(a, ((0, height - a.shape[0]), (0, 0)))


def _layer(head, x, first_key, lambda_init, norm1_g, w_in, mu, w0, w2, a0, a2, g2, k_k, k_a, r_k, lnx_w, lnx_b,
           q_gain, k_gain, lam_q1, lam_k1, lam_q2, lam_k2, subln_w, w_out, norm2_g, w_up, conv_w, conv_b,
           w_down):
    w = RWKV_WIDTH
    row = lambda a: a.reshape(1, -1)
    rwkv_cols = 3 * w + DECAY_LORA + AAA_LORA + GATE_LORA
    cuts = np.cumsum([3 * w, DECAY_LORA, AAA_LORA])
    wi_rkv, wi_wd, wi_ad, wi_gd = jnp.split(w_in[:, :rwkv_cols], cuts, axis=1)
    mu_rkv, mu_wd, mu_ad, mu_gd = jnp.split(row(mu), cuts, axis=1)
    pads = LORA_PAD
    w_proj = jnp.concatenate(
        [w_in[:, rwkv_cols:], wi_rkv, _pad_cols(wi_wd, pads[0]), _pad_cols(wi_ad, pads[1]),
         _pad_cols(wi_gd, pads[2])], axis=1).astype(BF16)
    mu_l = jnp.concatenate([_pad_cols(mu_wd, pads[0]), _pad_cols(mu_ad, pads[1]), _pad_cols(mu_gd, pads[2])],
                           axis=1)

    p = _proj(head, x, row(norm1_g), w_proj, PROJ_COLS // 4)

    at, rt, bt, kt, v, bht, kht, gct, g, bonus = _rwkv_pre(
        p, mu_rkv, mu_l, row(w0), _pad_rows(w2, pads[0]), row(a0), _pad_rows(a2, pads[1]),
        _pad_rows(g2, pads[2]), row(k_k), row(k_a), row(r_k))
    y_a = _rwkv_chunks(at, rt, bt, kt, v, bht, kht, gct, g, bonus, row(lnx_w), row(lnx_b))

    tile_heads = lambda a: jnp.tile(row(a), (1, 2 * DIFF_HEADS))
    qz, kn, vt = _attn_pre(p, tile_heads(q_gain), tile_heads(k_gain))
    lam_p = jnp.stack([lam_q1, lam_k1, lam_q2, lam_k2]).astype(F32)
    y_b = _flash(qz, kn, vt, lam_p, row(subln_w), first_key, lambda_init)

    w_out_b = w_out.astype(BF16)
    h2, hn2 = _out_proj(y_a, y_b, w_out_b[:w], w_out_b[w:], head, x, row(norm2_g), ROW_TILE // 2)

    return _ffn(hn2, h2, w_up.astype(BF16), conv_w, row(conv_b), w_down.astype(BF16), 1, x.shape[0], 512)


def kernel(x, meta, norm1_g, w_in, mu, w0, w2, a0, a2, g2, k_k, k_a, r_k, lnx_w, lnx_b, q_gain, k_gain,
           lam_q1, lam_k1, lam_q2, lam_k2, subln_w, w_out, norm2_g, w_up, conv_w, conv_b, w_down):
    batch, seq, d = x.shape
    depth = norm1_g.shape[0]
    assert depth == 1 and seq % ROW_TILE == 0 and meta.shape[0] == N_META
    front = ROW_TILE - N_META
    head = jnp.concatenate([jnp.zeros((front, d), x.dtype), meta.astype(x.dtype)], axis=0)
    lambda_init = 0.8 - 0.6 * math.exp(-0.3 * 0)
    outs = [
        _layer(head, x[bi], front, lambda_init, norm1_g[0], w_in[0], mu[0], w0[0], w2[0], a0[0], a2[0], g2[0],
               k_k[0], k_a[0], r_k[0], lnx_w[0], lnx_b[0], q_gain[0], k_gain[0], lam_q1[0], lam_k1[0],
               lam_q2[0], lam_k2[0], subln_w[0], w_out[0], norm2_g[0], w_up[0], conv_w[0], conv_b[0], w_down[0])
        for bi in range(batch)]
    return outs[0][None] if batch == 1 else jnp.stack(outs, axis=0)
```

```python
import functools
import math

import jax
import jax.numpy as jnp
import numpy as np
from jax import lax
from jax.experimental import pallas as pl
from jax.experimental.pallas import tpu as pltpu

F32 = jnp.float32
BF16 = jnp.bfloat16

N_META = 16
RWKV_HEADS = 16
RWKV_HEAD_DIM = 64
RWKV_WIDTH = RWKV_HEADS * RWKV_HEAD_DIM
DECAY_LORA = 64
AAA_LORA = 64
GATE_LORA = 160
RWKV_LN_EPS = 64e-5
DIFF_HEADS = 8
DIFF_QK_DIM = 64
DIFF_V_DIM = 128
DIFF_WIDTH = DIFF_HEADS * DIFF_V_DIM
CONV_WIDTH = 3
NORM_EPS = 1e-6
SUBLN_EPS = 1e-5

LANES = 128
ROW_TILE = 512
CHUNK = 64
RWKV_ROWS = 2 * CHUNK
LORA_SIZES = (DECAY_LORA, AAA_LORA, GATE_LORA)
LORA_PAD = (128, 128, 256)
LORA_COLS = sum(LORA_PAD)
RWKV_PROJ_COLS = 3 * RWKV_WIDTH + LORA_COLS
VMEM_LIMIT = 56 * 1024 * 1024


def _cparams(sem):
    return pltpu.CompilerParams(dimension_semantics=sem, vmem_limit_bytes=VMEM_LIMIT)


def _split_dot(a, b_bf16, parts):
    acc = None
    rem = a
    for _ in range(parts):
        hi = rem.astype(BF16)
        term = jnp.dot(hi, b_bf16, preferred_element_type=F32)
        acc = term if acc is None else acc + term
        rem = rem - hi.astype(F32)
    return acc


def _dot3(a, b_hi, b_lo):
    ah = a.astype(BF16)
    al = (a - ah.astype(F32)).astype(BF16)
    d = functools.partial(jnp.dot, preferred_element_type=F32)
    return d(ah, b_hi) + (d(ah, b_lo) + d(al, b_hi))


def _group_ones(n, group):
    r = lax.broadcasted_iota(jnp.int32, (n, n), 0) // group
    c = lax.broadcasted_iota(jnp.int32, (n, n), 1) // group
    return (r == c).astype(BF16)


def _group_sum(x, group):
    ones = _group_ones(LANES, group)
    cols = [
        _split_dot(x[:, c:c + LANES], ones, 2)
        for c in range(0, x.shape[1], LANES)
    ]
    return jnp.concatenate(cols, axis=1)


def _proj_body(head_ref, x_ref, g_ref, w_ref, o_ref, hn_ref):
    @pl.when(pl.program_id(1) == 0)
    def _():
        x = jnp.where(pl.program_id(0) == 0, head_ref[...], x_ref[...])
        ms = jnp.mean(x * x, axis=-1, keepdims=True)
        hn_ref[...] = (x * lax.rsqrt(ms + NORM_EPS) * g_ref[...]).astype(BF16)

    o_ref[...] = jnp.dot(hn_ref[...], w_ref[...], preferred_element_type=F32)


def _proj(head, x, gain, w, col_tile):
    d = x.shape[1]
    tp = head.shape[0] + x.shape[0]
    n = w.shape[1]
    return pl.pallas_call(
        _proj_body,
        grid=(tp // ROW_TILE, n // col_tile),
        in_specs=[
            pl.BlockSpec((ROW_TILE, d), lambda i, j: (0, 0)),
            pl.BlockSpec((ROW_TILE, d), lambda i, j: (jnp.maximum(i - 1, 0), 0)),
            pl.BlockSpec((1, d), lambda i, j: (0, 0)),
            pl.BlockSpec((d, col_tile), lambda i, j: (0, j)),
        ],
        out_specs=pl.BlockSpec((ROW_TILE, col_tile), lambda i, j: (i, j)),
        out_shape=jax.ShapeDtypeStruct((tp, n), F32),
        scratch_shapes=[pltpu.VMEM((ROW_TILE, d), BF16)],
        compiler_params=_cparams(("arbitrary", "arbitrary")),
        name="proj",
    )(head, x, gain, w)


def _softplus(z):
    return jnp.maximum(z, 0.0) + jnp.log(1.0 + jnp.exp(-jnp.abs(z)))


def _sigmoid(z):
    return 0.5 * jnp.tanh(0.5 * z) + 0.5


def _rwkv_pre_body(r_ref, k_ref, v_ref, l_ref, mu_ref, mul_ref, w0_ref, w2h_ref, w2l_ref, a0_ref, a2h_ref,
                   a2l_ref, g2h_ref, g2l_ref, kk_ref, ka_ref, rk_ref,
                   at_ref, rt_ref, bt_ref, kt_ref, vo_ref, bht_ref, kht_ref, gct_ref, g_ref, bonus_ref,
                   last_ref, lastl_ref):
    i = pl.program_id(0)

    @pl.when(i == 0)
    def _():
        last_ref[...] = jnp.zeros_like(last_ref)
        lastl_ref[...] = jnp.zeros_like(lastl_ref)

    rows = r_ref.shape[0]
    row = lax.broadcasted_iota(jnp.int32, (rows, 1), 0)

    def shift_mix(p, last_row, mu):
        prev = jnp.where(row == 0, last_row, pltpu.roll(p, 1, axis=0))
        return p + (prev - p) * mu

    w = RWKV_WIDTH
    r_raw, k_raw, v_raw, l_raw = r_ref[...], k_ref[...], v_ref[...], l_ref[...]
    r = shift_mix(r_raw, last_ref[0:1, 0:w], mu_ref[:, 0:w])
    k = shift_mix(k_raw, last_ref[0:1, w:2 * w], mu_ref[:, w:2 * w])
    v = shift_mix(v_raw, last_ref[0:1, 2 * w:3 * w], mu_ref[:, 2 * w:3 * w])
    lo = shift_mix(l_raw, lastl_ref[0:1, :], mul_ref[...])
    last_ref[0:1, 0:w] = r_raw[rows - 1:rows]
    last_ref[0:1, w:2 * w] = k_raw[rows - 1:rows]
    last_ref[0:1, 2 * w:3 * w] = v_raw[rows - 1:rows]
    lastl_ref[0:1, :] = l_raw[rows - 1:rows]

    c0, c1 = LORA_PAD[0], LORA_PAD[0] + LORA_PAD[1]
    wd, ad, gd = lo[:, 0:c0], lo[:, c0:c1], lo[:, c1:]
    wlog = -_softplus(-(w0_ref[...] + _dot3(jnp.tanh(wd), w2h_ref[...], w2l_ref[...]))) - 0.5
    logd = -jnp.exp(wlog)
    a = _sigmoid(a0_ref[...] + _dot3(ad, a2h_ref[...], a2l_ref[...]))
    g_ref[...] = _dot3(_sigmoid(gd), g2h_ref[...], g2l_ref[...])

    kk = k * kk_ref[...]
    kk = kk * jnp.minimum(lax.rsqrt(_group_sum(kk * kk, RWKV_HEAD_DIM)), 1e12)
    k = k * (1.0 + (a - 1.0) * ka_ref[...])
    b = kk * a
    bonus_ref[...] = _group_sum(r * k * rk_ref[...], RWKV_HEAD_DIM) * v

    tr = lax.broadcasted_iota(jnp.int32, (rows, rows), 0)
    tc = lax.broadcasted_iota(jnp.int32, (rows, rows), 1)
    same = (tr // CHUNK) == (tc // CHUNK)
    cum, tot = _split_dot_lhs([(same & (tc <= tr)).astype(BF16), same.astype(BF16)], logd)

    e_prev = jnp.exp(cum - logd)
    e_cum = jnp.exp(cum)
    e_neg = jnp.exp(-cum)
    e_rem = jnp.exp(tot - cum)
    at_ref[...] = (-kk * e_prev).astype(BF16)
    rt_ref[...] = (r * e_cum).astype(BF16)
    bt_ref[...] = (b * e_neg).astype(BF16)
    kt_ref[...] = (k * e_neg).astype(BF16)
    vo_ref[...] = v.astype(BF16)
    bht_ref[...] = (b * e_rem).T.astype(BF16)
    kht_ref[...] = (k * e_rem).T.astype(BF16)
    gct_ref[...] = jnp.exp(tot).T


def _split_dot_lhs(lhs_list, b):
    accs = [None] * len(lhs_list)
    rem = b
    for _ in range(3):
        hi = rem.astype(BF16)
        for n, a in enumerate(lhs_list):
            term = jnp.dot(a, hi, preferred_element_type=F32)
            accs[n] = term if accs[n] is None else accs[n] + term
        rem = rem - hi.astype(F32)
    return accs


def _rwkv_pre(p, mu, mul, w0, w2, a0, a2, g2, k_k, k_a, r_k):
    tp = p.shape[0]
    w = RWKV_WIDTH
    rows = RWKV_ROWS
    lora_blk = 3 * w // LORA_COLS

    def full(arr):
        return pl.BlockSpec(arr.shape, lambda i: (0, 0))

    tok = pl.BlockSpec((rows, w), lambda i: (i, 0))
    chan = pl.BlockSpec((w, rows), lambda i: (0, i))
    tok_bf = jax.ShapeDtypeStruct((tp, w), BF16)
    params = (mu, mul, w0, *w2, a0, *a2, *g2, k_k, k_a, r_k)
    return pl.pallas_call(
        _rwkv_pre_body,
        grid=(tp // rows,),
        in_specs=[
            pl.BlockSpec((rows, w), lambda i: (i, 0)),
            pl.BlockSpec((rows, w), lambda i: (i, 1)),
            pl.BlockSpec((rows, w), lambda i: (i, 2)),
            pl.BlockSpec((rows, LORA_COLS), lambda i: (i, lora_blk)),
            *[full(a) for a in params],
        ],
        out_specs=[tok, tok, tok, tok, tok, chan, chan, chan, tok, tok],
        out_shape=[tok_bf, tok_bf, tok_bf, tok_bf, tok_bf,
                   jax.ShapeDtypeStruct((w, tp), BF16), jax.ShapeDtypeStruct((w, tp), BF16),
                   jax.ShapeDtypeStruct((w, tp), F32),
                   jax.ShapeDtypeStruct((tp, w), F32), jax.ShapeDtypeStruct((tp, w), F32)],
        scratch_shapes=[pltpu.VMEM((8, 3 * w), F32), pltpu.VMEM((8, LORA_COLS), F32)],
        compiler_params=_cparams(("arbitrary",)),
        name="rwkv_pre",
    )(p, p, p, p, *params)


def _rwkv_chunk_body(at_ref, rt_ref, bt_ref, kt_ref, v_ref, bht_ref, kht_ref, gct_ref, g_ref, bonus_ref,
                     lnw_ref, lnb_ref, y_ref, state_ref, yn_ref):
    n = RWKV_HEAD_DIM
    c = CHUNK

    @pl.when(pl.program_id(0) == 0)
    def _():
        state_ref[...] = jnp.zeros_like(state_ref)

    tr = lax.broadcasted_iota(jnp.int32, (2 * c, 2 * c), 0)
    tc = lax.broadcasted_iota(jnp.int32, (2 * c, 2 * c), 1) % c
    mask = ((tr < c) & (tc < tr)) | ((tr >= c) & (tc <= tr - c))
    er = lax.broadcasted_iota(jnp.int32, (c, 2 * c), 0)
    ec = lax.broadcasted_iota(jnp.int32, (c, 2 * c), 1)
    right = ec >= c
    eye2 = (ec == er + c).astype(F32)
    zeros_cn = jnp.zeros((c, n), BF16)
    dot = functools.partial(jnp.dot, preferred_element_type=F32)
    bf = lambda t: t.astype(BF16)

    heads = range(RWKV_HEADS)
    chunks = range(at_ref.shape[0] // c)
    units = [(ci, h) for ci in chunks for h in heads]

    def blk(ref, u):
        return ref[u[0] * c:(u[0] + 1) * c, u[1] * n:(u[1] + 1) * n]

    lhs = {u: jnp.concatenate([blk(at_ref, u), blk(rt_ref, u)], axis=0) for u in units}
    w = {}
    for u in units:
        rhs = jnp.concatenate([blk(bt_ref, u), blk(kt_ref, u)], axis=0)
        wu = lax.dot_general(lhs[u], rhs, (((1,), (1,)), ((), ())), preferred_element_type=F32)
        w[u] = jnp.where(mask, wu, 0.0)
    top = {u: bf(w[u][:c]) for u in units}
    bot = {u: bf(w[u][c:]) for u in units}
    xs = {u: jnp.where(right, eye2, w[u][:c]) for u in units}
    for _ in range(int(math.log2(c))):
        xb = {u: bf(xs[u]) for u in units}
        xs = {u: dot(xb[u][:, :c], xb[u]) + jnp.where(right, xs[u], 0.0) for u in units}
    invb = {u: bf(xs[u]) for u in units}

    for ci in chunks:
        us = [(ci, h) for h in heads]
        h0 = [state_ref[h] for h in heads]
        x0 = [dot(lhs[u], bf(h0[u[1]])) for u in us]
        v = [blk(v_ref, u) for u in us]
        sys_rhs = [x0[h][:c] + dot(top[(ci, h)], jnp.concatenate([zeros_cn, v[h]], axis=0)) for h in heads]
        u_sol = [dot(invb[(ci, h)], jnp.concatenate([zeros_cn, bf(sys_rhs[h])], axis=0)) for h in heads]
        uv = [jnp.concatenate([bf(u_sol[h]), v[h]], axis=0) for h in heads]
        for h in heads:
            rs = slice(ci * c, (ci + 1) * c)
            cs = slice(h * n, (h + 1) * n)
            yn_ref[rs, cs] = x0[h][c:] + dot(bot[(ci, h)], uv[h])
            bkt = jnp.concatenate([bht_ref[cs, rs], kht_ref[cs, rs]], axis=1)
            state_ref[h] = gct_ref[cs, rs] * h0[h] + dot(bkt, uv[h])

    y = yn_ref[...]
    inv_n = 1.0 / n
    d = y - _group_sum(y, n) * inv_n
    var = _group_sum(d * d, n) * inv_n
    yn = d * lax.rsqrt(var + RWKV_LN_EPS)
    y_ref[...] = ((yn * lnw_ref[...] + lnb_ref[...] + bonus_ref[...]) * g_ref[...]).astype(BF16)


def _rwkv_chunks(at, rt, bt, kt, v, bht, kht, gct, g, bonus, lnw, lnb):
    tp, w = at.shape
    rows = RWKV_ROWS
    tok = pl.BlockSpec((rows, w), lambda i: (i, 0))
    chan = pl.BlockSpec((w, rows), lambda i: (0, i))
    vec = pl.BlockSpec((1, w), lambda i: (0, 0))
    return pl.pallas_call(
        _rwkv_chunk_body,
        grid=(tp // rows,),
        in_specs=[tok, tok, tok, tok, tok, chan, chan, chan, tok, tok, vec, vec],
        out_specs=tok,
        out_shape=jax.ShapeDtypeStruct((tp, w), BF16),
        scratch_shapes=[pltpu.VMEM((RWKV_HEADS, RWKV_HEAD_DIM, RWKV_HEAD_DIM), F32),
                        pltpu.VMEM((rows, w), F32)],
        compiler_params=_cparams(("arbitrary",)),
        name="rwkv_chunks",
    )(at, rt, bt, kt, v, bht, kht, gct, g, bonus, lnw, lnb)


def _attn_pre_body(q_ref, k_ref, v_ref, qg_ref, kg_ref, qz_ref, kn_ref, vt_ref):
    d = DIFF_QK_DIM

    def qk_norm(x, gain):
        ms = _group_sum(x * x, d) * (1.0 / d)
        return x * lax.rsqrt(ms + NORM_EPS) * gain

    q = qk_norm(q_ref[...], qg_ref[...]) * (d ** -0.5 * math.log2(math.e))
    first = (lax.broadcasted_iota(jnp.int32, q.shape, 1) % (2 * d)) < d
    qz_ref[0] = jnp.where(first, q, 0.0).astype(BF16)
    qz_ref[1] = jnp.where(first, 0.0, q).astype(BF16)
    kn_ref[...] = qk_norm(k_ref[...], kg_ref[...]).astype(BF16)
    vt_ref[...] = v_ref[...].T.astype(BF16)


def _attn_pre(p, q_gain, k_gain):
    tp = p.shape[0]
    w = DIFF_WIDTH
    vec = pl.BlockSpec((1, w), lambda i: (0, 0))
    blk = lambda j: pl.BlockSpec((ROW_TILE, w), lambda i: (i, j))
    return pl.pallas_call(
        _attn_pre_body,
        grid=(tp // ROW_TILE,),
        in_specs=[blk(0), blk(1), blk(2), vec, vec],
        out_specs=[pl.BlockSpec((2, ROW_TILE, w), lambda i: (0, i, 0)), blk(0),
                   pl.BlockSpec((w, ROW_TILE), lambda i: (0, i))],
        out_shape=[jax.ShapeDtypeStruct((2, tp, w), BF16), jax.ShapeDtypeStruct((tp, w), BF16),
                   jax.ShapeDtypeStruct((w, tp), BF16)],
        compiler_params=_cparams(("arbitrary",)),
        name="attn_pre",
    )(p, p, p, q_gain, k_gain)


FLASH_HEADS = 4
FLASH_QCOLS = 256


def _flash_body(qi_ref, ki_ref, qz_ref, k_ref, vt_ref, lam_ref, sw_ref, o_ref, m_ref, l_ref, acc_ref,
                *, first_key, lambda_init):
    step = pl.program_id(1)
    qi = qi_ref[step]
    ki = ki_ref[step]
    tq = o_ref.shape[0]
    tk = k_ref.shape[0]
    e = DIFF_V_DIM
    heads = range(FLASH_HEADS)
    lanes = [slice(g * e, (g + 1) * e) for g in heads]

    @pl.when(ki == 0)
    def _():
        m_ref[...] = jnp.full_like(m_ref, -jnp.inf)
        l_ref[...] = jnp.zeros_like(l_ref)
        acc_ref[...] = jnp.zeros_like(acc_ref)

    def strand(g, c0, rows, masked):
        cols = slice(c0, c0 + FLASH_QCOLS)
        which, off = divmod(c0, tq)
        n = rows.stop - rows.start
        st = {}

        def stage_scores():
            q = qz_ref[which, off:off + FLASH_QCOLS, lanes[g]]
            s = lax.dot_general(k_ref[rows, lanes[g]], q, (((1,), (1,)), ((), ())),
                                preferred_element_type=F32)
            if masked:
                k_pos = ki * tk + rows.start + lax.broadcasted_iota(jnp.int32, (n, FLASH_QCOLS), 0)
                q_pos = qi * tq + off + lax.broadcasted_iota(jnp.int32, (n, FLASH_QCOLS), 1)
                vis = (k_pos <= q_pos) & ((k_pos >= first_key) | (q_pos < first_key))
                s = jnp.where(vis, s, -jnp.inf)
            st["s"] = s

        def stage_softmax():
            m_prev = m_ref[g, :, cols]
            m_new = jnp.maximum(m_prev, jnp.max(st["s"], axis=0, keepdims=True))
            st["alpha"] = jnp.exp2(m_prev - m_new)
            p = jnp.exp2(st["s"] - m_new)
            l_ref[g, :, cols] = st["alpha"] * l_ref[g, :, cols] + jnp.sum(p, axis=0, keepdims=True)
            m_ref[g, :, cols] = m_new
            st["p"] = p.astype(BF16)

        def stage_values():
            acc_ref[g, :, cols] = st["alpha"] * acc_ref[g, :, cols] + jnp.dot(
                vt_ref[lanes[g], rows], st["p"], preferred_element_type=F32)

        return [stage_scores, stage_softmax, stage_values]

    def run(rows_for, masked):
        strands = [strand(g, c0, rows_for(c0 % tq), masked)
                   for c0 in range(0, 2 * tq, FLASH_QCOLS) for g in heads]
        depth = len(strands[0])
        for t in range(len(strands) + depth - 1):
            for stage in range(depth):
                if 0 <= t - stage < len(strands):
                    strands[t - stage][stage]()

    everything = slice(0, tk)
    tail = slice(first_key // LANES * LANES, tk)

    @pl.when(ki == qi)
    def _():
        run(lambda off: slice(0, off + FLASH_QCOLS), True)

    @pl.when((ki == 0) & (qi > 0))
    def _():
        run(lambda off: tail, True)

    @pl.when((ki > 0) & (ki < qi))
    def _():
        run(lambda off: everything, False)

    @pl.when(ki == qi)
    def _():
        lp = lam_ref[...]
        lam = (jnp.exp(jnp.sum(lp[0:1] * lp[1:2])) - jnp.exp(jnp.sum(lp[2:3] * lp[3:4])) + lambda_init)
        for g in heads:
            o = acc_ref[g] / l_ref[g]
            o = o[:, :tq] - lam * o[:, tq:]
            ms = jnp.mean(o * o, axis=0, keepdims=True)
            o = (o * lax.rsqrt(ms + SUBLN_EPS)).T * (sw_ref[...] * (1.0 - lambda_init))
            o_ref[:, lanes[g]] = o.astype(BF16)


def _flash(qz, kn, vt, lam_p, subln_w, first_key, lambda_init):
    _, tp, w = qz.shape
    e = DIFF_V_DIM
    we = FLASH_HEADS * e
    nb = tp // ROW_TILE
    pairs = [(qi, ki) for qi in range(nb) for ki in range(qi + 1)]
    qi_tab = jnp.asarray(np.array([p[0] for p in pairs], np.int32))
    ki_tab = jnp.asarray(np.array([p[1] for p in pairs], np.int32))
    grid_spec = pltpu.PrefetchScalarGridSpec(
        num_scalar_prefetch=2,
        grid=(DIFF_HEADS // FLASH_HEADS, len(pairs)),
        in_specs=[
            pl.BlockSpec((2, ROW_TILE, we), lambda h, s, qt, kt: (0, qt[s], h)),
            pl.BlockSpec((ROW_TILE, we), lambda h, s, qt, kt: (kt[s], h)),
            pl.BlockSpec((we, ROW_TILE), lambda h, s, qt, kt: (h, kt[s])),
            pl.BlockSpec(lam_p.shape, lambda h, s, qt, kt: (0, 0)),
            pl.BlockSpec((1, e), lambda h, s, qt, kt: (0, 0)),
        ],
        out_specs=pl.BlockSpec((ROW_TILE, we), lambda h, s, qt, kt: (qt[s], h)),
        scratch_shapes=[pltpu.VMEM((FLASH_HEADS, 1, 2 * ROW_TILE), F32),
                        pltpu.VMEM((FLASH_HEADS, 1, 2 * ROW_TILE), F32),
                        pltpu.VMEM((FLASH_HEADS, e, 2 * ROW_TILE), F32)],
    )
    return pl.pallas_call(
        functools.partial(_flash_body, first_key=first_key, lambda_init=lambda_init),
        grid_spec=grid_spec,
        out_shape=jax.ShapeDtypeStruct((tp, w), BF16),
        compiler_params=_cparams(("arbitrary", "arbitrary")),
        name="diff_flash",
    )(qi_tab, ki_tab, qz, kn, vt, lam_p, subln_w)


def _out_proj_body(ya_ref, yb_ref, wa_ref, wb_ref, head_ref, x_ref, g_ref, h2_ref, hn_ref, *, head_blocks):
    acc = jnp.dot(ya_ref[...], wa_ref[...], preferred_element_type=F32)
    acc = acc + jnp.dot(yb_ref[...], wb_ref[...], preferred_element_type=F32)
    h2 = jnp.where(pl.program_id(0) < head_blocks, head_ref[...], x_ref[...]) + acc
    h2_ref[...] = h2
    ms = jnp.mean(h2 * h2, axis=-1, keepdims=True)
    hn_ref[...] = (h2 * lax.rsqrt(ms + NORM_EPS) * g_ref[...]).astype(BF16)


def _out_proj(ya, yb, wa, wb, head, x, gain, row_tile):
    d = x.shape[1]
    tp = head.shape[0] + x.shape[0]
    nhb = head.shape[0] // row_tile
    rows = pl.BlockSpec((row_tile, d), lambda i: (i, 0))
    return pl.pallas_call(
        functools.partial(_out_proj_body, head_blocks=nhb),
        grid=(tp // row_tile,),
        in_specs=[
            pl.BlockSpec((row_tile, ya.shape[1]), lambda i: (i, 0)),
            pl.BlockSpec((row_tile, yb.shape[1]), lambda i: (i, 0)),
            pl.BlockSpec(wa.shape, lambda i: (0, 0)),
            pl.BlockSpec(wb.shape, lambda i: (0, 0)),
            pl.BlockSpec((row_tile, d), lambda i: (jnp.minimum(i, nhb - 1), 0)),
            pl.BlockSpec((row_tile, d), lambda i: (jnp.maximum(i - nhb, 0), 0)),
            pl.BlockSpec((1, d), lambda i: (0, 0)),
        ],
        out_specs=[rows, rows],
        out_shape=[jax.ShapeDtypeStruct((tp, d), F32), jax.ShapeDtypeStruct((tp, d), BF16)],
        compiler_params=_cparams(("arbitrary",)),
        name="out_proj",
    )(ya, yb, wa, wb, head, x, gain)


FFN_SUB = 256


def _ffn_body(x_ref, wg_ref, wu_ref, cwg_ref, cwu_ref, cbg_ref, cbu_ref, wd_ref, h2_ref, o_ref,
              acc_ref, ua_ref, ub_ref, *, nf):
    s = pl.program_id(0)
    halo = x_ref.shape[0] - o_ref.shape[0]
    tile = jnp.maximum(s - 1, 0) % nf
    subs = [slice(c, c + FFN_SUB) for c in range(0, wg_ref.shape[1], FFN_SUB)]

    @pl.when(s == 0)
    def _():
        ua_ref[...] = jnp.zeros_like(ua_ref)
        ub_ref[...] = jnp.zeros_like(ub_ref)

    def conv(u, cw_ref, cb_ref, cols):
        cw = cw_ref[:, cols]
        u1 = pltpu.roll(u, 1, axis=0)
        y = pltpu.roll(u1, 1, axis=0) * cw[0:1] + u1 * cw[1:2] + u * cw[2:3]
        return y[halo:] + cb_ref[:, cols]

    def step(u_next, u_prev):
        x = x_ref[...]
        for c in subs:
            u_next[0, :, c] = jnp.dot(x, wg_ref[:, c], preferred_element_type=F32)
            u_next[1, :, c] = jnp.dot(x, wu_ref[:, c], preferred_element_type=F32)
        part = None
        for c in subs:
            gate = conv(u_prev[0, :, c], cwg_ref, cbg_ref, c)
            up = conv(u_prev[1, :, c], cwu_ref, cbu_ref, c)
            act = (gate * _sigmoid(gate) * up).astype(BF16)
            term = jnp.dot(act, wd_ref[c, :], preferred_element_type=F32)
            part = term if part is None else part + term
        acc_ref[...] = jnp.where(tile == 0, part, acc_ref[...] + part)

    @pl.when(s % 2 == 0)
    def _():
        step(ua_ref, ub_ref)

    @pl.when(s % 2 == 1)
    def _():
        step(ub_ref, ua_ref)

    @pl.when((s > 0) & (tile == nf - 1))
    def _():
        o_ref[...] = h2_ref[...] + acc_ref[...]


def _ffn(hn, h2, w_up, conv_w, conv_b, w_down, first_block, n_out_rows, ff_tile):
    tp, d = hn.shape
    dff = w_down.shape[0]
    nf = dff // ff_tile
    halo = 16
    fb = first_block
    items = (n_out_rows // ROW_TILE) * nf

    def up_item(s):
        it = jnp.minimum(s, items - 1)
        return it // nf, it % nf

    def down_item(s):
        it = jnp.maximum(s - 1, 0)
        return it // nf, it % nf

    return pl.pallas_call(
        functools.partial(_ffn_body, nf=nf),
        grid=(items + 1,),
        in_specs=[
            pl.BlockSpec((pl.Element(ROW_TILE + halo), pl.Element(d)),
                         lambda s: (pl.multiple_of((up_item(s)[0] + fb) * ROW_TILE - halo, halo), 0)),
            pl.BlockSpec((d, ff_tile), lambda s: (0, up_item(s)[1])),
            pl.BlockSpec((d, ff_tile), lambda s: (0, up_item(s)[1] + nf)),
            pl.BlockSpec((CONV_WIDTH, ff_tile), lambda s: (0, down_item(s)[1])),
            pl.BlockSpec((CONV_WIDTH, ff_tile), lambda s: (0, down_item(s)[1] + nf)),
            pl.BlockSpec((1, ff_tile), lambda s: (0, down_item(s)[1])),
            pl.BlockSpec((1, ff_tile), lambda s: (0, down_item(s)[1] + nf)),
            pl.BlockSpec((ff_tile, d), lambda s: (down_item(s)[1], 0)),
            pl.BlockSpec((ROW_TILE, d), lambda s: (down_item(s)[0] + fb, 0)),
        ],
        out_specs=pl.BlockSpec((ROW_TILE, d), lambda s: (down_item(s)[0], 0)),
        out_shape=jax.ShapeDtypeStruct((n_out_rows, d), F32),
        scratch_shapes=[pltpu.VMEM((ROW_TILE, d), F32),
                        pltpu.VMEM((2, ROW_TILE + halo, ff_tile), F32),
                        pltpu.VMEM((2, ROW_TILE + halo, ff_tile), F32)],
        compiler_params=_cparams(("arbitrary",)),
        name="conv_glu_ffn",
    )(hn, w_up, w_up, conv_w, conv_w, conv_b, conv_b, w_down, h2)


def _pad_cols(a, width):
    return jnp.pad(a, ((0, 0), (0, width - a.shape[1])))


def _layer(head, x, first_key, lambda_init, norm1_g, w_in, mu, w0, w2, a0, a2, g2, k_k, k_a, r_k, lnx_w, lnx_b,
           q_gain, k_gain, lam_q1, lam_k1, lam_q2, lam_k2, subln_w, w_out, norm2_g, w_up, conv_w, conv_b,
           w_down):
    w = RWKV_WIDTH
    row = lambda a: a.reshape(1, -1)
    n_rkv = 3 * w
    cuts = np.cumsum((n_rkv,) + LORA_SIZES)
    segs = [w_in[:, :n_rkv]] + [_pad_cols(w_in[:, lo:hi], wd) for lo, hi, wd in zip(cuts[:-1], cuts[1:], LORA_PAD)]
    w_a = jnp.concatenate(segs, axis=1).astype(BF16)
    w_b = w_in[:, cuts[-1]:].astype(BF16)
    mu_row = row(mu)
    mu_l = jnp.concatenate([_pad_cols(mu_row[:, lo:hi], wd) for lo, hi, wd in zip(cuts[:-1], cuts[1:], LORA_PAD)],
                           axis=1)

    def lora_weight(wt, rows):
        full = jnp.pad(wt, ((0, rows - wt.shape[0]), (0, 0)))
        hi = full.astype(BF16)
        return hi, (full - hi.astype(F32)).astype(BF16)

    p_a = _proj(head, x, row(norm1_g), w_a, RWKV_PROJ_COLS // 2)
    p_b = _proj(head, x, row(norm1_g), w_b, 3 * DIFF_WIDTH // 2)

    at, rt, bt, kt, v, bht, kht, gct, g, bonus = _rwkv_pre(
        p_a, mu_row[:, :n_rkv], mu_l, row(w0), lora_weight(w2, LORA_PAD[0]), row(a0),
        lora_weight(a2, LORA_PAD[1]), lora_weight(g2, LORA_PAD[2]), row(k_k), row(k_a), row(r_k))
    y_a = _rwkv_chunks(at, rt, bt, kt, v, bht, kht, gct, g, bonus, row(lnx_w), row(lnx_b))

    tile_heads = lambda a: jnp.tile(row(a), (1, 2 * DIFF_HEADS))
    qz, kn, vt = _attn_pre(p_b, tile_heads(q_gain), tile_heads(k_gain))
    lam_p = jnp.stack([lam_q1, lam_k1, lam_q2, lam_k2]).astype(F32)
    y_b = _flash(qz, kn, vt, lam_p, row(subln_w), first_key, lambda_init)

    w_out_b = w_out.astype(BF16)
    h2, hn2 = _out_proj(y_a, y_b, w_out_b[:w], w_out_b[w:], head, x, row(norm2_g), ROW_TILE)

    return _ffn(hn2, h2, w_up.astype(BF16), conv_w, row(conv_b), w_down.astype(BF16), 1, x.shape[0], 512)


def kernel(x, meta, norm1_g, w_in, mu, w0, w2, a0, a2, g2, k_k, k_a, r_k, lnx_w, lnx_b, q_gain, k_gain,
           lam_q1, lam_k1, lam_q2, lam_k2, subln_w, w_out, norm2_g, w_up, conv_w, conv_b, w_down):
    batch, seq, d = x.shape
    depth = norm1_g.shape[0]
    assert depth == 1 and seq % ROW_TILE == 0 and meta.shape[0] == N_META
    front = ROW_TILE - N_META
    head = jnp.concatenate([jnp.zeros((front, d), x.dtype), meta.astype(x.dtype)], axis=0)
    lambda_init = 0.8 - 0.6 * math.exp(-0.3 * 0)
    outs = [
        _layer(head, x[bi], front, lambda_init, norm1_g[0], w_in[0], mu[0], w0[0], w2[0], a0[0], a2[0], g2[0],
               k_k[0], k_a[0], r_k[0], lnx_w[0], lnx_b[0], q_gain[0], k_gain[0], lam_q1[0], lam_k1[0],
               lam_q2[0], lam_k2[0], subln_w[0], w_out[0], norm2_g[0], w_up[0], conv_w[0], conv_b[0], w_down[0])
        for bi in range(batch)]
    return outs[0][None] if batch == 1 else jnp.stack(outs, axis=0)
```

```python
import functools
import math

import jax
import jax.numpy as jnp
import numpy as np
from jax import lax
from jax.experimental import pallas as pl
from jax.experimental.pallas import tpu as pltpu

F32 = jnp.float32
BF16 = jnp.bfloat16

N_META = 16
RWKV_HEADS = 16
RWKV_HEAD_DIM = 64
RWKV_WIDTH = RWKV_HEADS * RWKV_HEAD_DIM
DECAY_LORA = 64
AAA_LORA = 64
GATE_LORA = 160
RWKV_LN_EPS = 64e-5
DIFF_HEADS = 8
DIFF_QK_DIM = 64
DIFF_V_DIM = 128
DIFF_WIDTH = DIFF_HEADS * DIFF_V_DIM
CONV_WIDTH = 3
NORM_EPS = 1e-6
SUBLN_EPS = 1e-5

LANES = 128
ROW_TILE = 512
CHUNK = 64
RWKV_ROWS = 4 * CHUNK
LORA_SIZES = (DECAY_LORA, AAA_LORA, GATE_LORA)
LORA_PAD = (128, 128, 256)
LORA_COLS = sum(LORA_PAD)
MXU_COLS = 256
PROJ_COLS = -(-(3 * DIFF_WIDTH + 3 * RWKV_WIDTH + LORA_COLS) // (3 * MXU_COLS)) * (3 * MXU_COLS)
VMEM_LIMIT = 56 * 1024 * 1024


def _cparams(sem):
    return pltpu.CompilerParams(dimension_semantics=sem, vmem_limit_bytes=VMEM_LIMIT)


def _split_dot(a, b_bf16, parts):
    acc = None
    rem = a
    for _ in range(parts):
        hi = rem.astype(BF16)
        term = jnp.dot(hi, b_bf16, preferred_element_type=F32)
        acc = term if acc is None else acc + term
        rem = rem - hi.astype(F32)
    return acc


def _dot3(a, b_hi, b_lo):
    ah = a.astype(BF16)
    al = (a - ah.astype(F32)).astype(BF16)
    d = functools.partial(jnp.dot, preferred_element_type=F32)
    return d(ah, b_hi) + (d(ah, b_lo) + d(al, b_hi))


def _group_ones(n, group):
    r = lax.broadcasted_iota(jnp.int32, (n, n), 0) // group
    c = lax.broadcasted_iota(jnp.int32, (n, n), 1) // group
    return (r == c).astype(BF16)


def _group_sum(x, group):
    ones = _group_ones(LANES, group)
    cols = [
        _split_dot(x[:, c:c + LANES], ones, 2)
        for c in range(0, x.shape[1], LANES)
    ]
    return jnp.concatenate(cols, axis=1)


def _proj_body(head_ref, x_ref, g_ref, w_ref, o_ref, hn_ref):
    @pl.when(pl.program_id(1) == 0)
    def _():
        x = jnp.where(pl.program_id(0) == 0, head_ref[...], x_ref[...])
        ms = jnp.mean(x * x, axis=-1, keepdims=True)
        hn_ref[...] = (x * lax.rsqrt(ms + NORM_EPS) * g_ref[...]).astype(BF16)

    o_ref[...] = jnp.dot(hn_ref[...], w_ref[...], preferred_element_type=F32)


def _proj(head, x, gain, w, col_tile):
    d = x.shape[1]
    tp = head.shape[0] + x.shape[0]
    n = w.shape[1]
    return pl.pallas_call(
        _proj_body,
        grid=(tp // ROW_TILE, n // col_tile),
        in_specs=[
            pl.BlockSpec((ROW_TILE, d), lambda i, j: (0, 0)),
            pl.BlockSpec((ROW_TILE, d), lambda i, j: (jnp.maximum(i - 1, 0), 0)),
            pl.BlockSpec((1, d), lambda i, j: (0, 0)),
            pl.BlockSpec((d, col_tile), lambda i, j: (0, j)),
        ],
        out_specs=pl.BlockSpec((ROW_TILE, col_tile), lambda i, j: (i, j)),
        out_shape=jax.ShapeDtypeStruct((tp, n), F32),
        scratch_shapes=[pltpu.VMEM((ROW_TILE, d), BF16)],
        compiler_params=_cparams(("arbitrary", "arbitrary")),
        name="proj",
    )(head, x, gain, w)


def _softplus(z):
    return jnp.maximum(z, 0.0) + jnp.log(1.0 + jnp.exp(-jnp.abs(z)))


def _sigmoid(z):
    return 0.5 * jnp.tanh(0.5 * z) + 0.5


def _rwkv_pre_body(r_ref, k_ref, v_ref, l_ref, mu_ref, mul_ref, w0_ref, w2h_ref, w2l_ref, a0_ref, a2h_ref,
                   a2l_ref, g2h_ref, g2l_ref, kk_ref, ka_ref, rk_ref,
                   at_ref, rt_ref, bt_ref, kt_ref, vo_ref, bht_ref, kht_ref, gct_ref, g_ref, bonus_ref,
                   last_ref, lastl_ref):
    i = pl.program_id(0)

    @pl.when(i == 0)
    def _():
        last_ref[...] = jnp.zeros_like(last_ref)
        lastl_ref[...] = jnp.zeros_like(lastl_ref)

    rows = r_ref.shape[0]
    row = lax.broadcasted_iota(jnp.int32, (rows, 1), 0)

    def shift_mix(p, last_row, mu):
        prev = jnp.where(row == 0, last_row, pltpu.roll(p, 1, axis=0))
        return p + (prev - p) * mu

    w = RWKV_WIDTH
    r_raw, k_raw, v_raw, l_raw = r_ref[...], k_ref[...], v_ref[...], l_ref[...]
    r = shift_mix(r_raw, last_ref[0:1, 0:w], mu_ref[:, 0:w])
    k = shift_mix(k_raw, last_ref[0:1, w:2 * w], mu_ref[:, w:2 * w])
    v = shift_mix(v_raw, last_ref[0:1, 2 * w:3 * w], mu_ref[:, 2 * w:3 * w])
    lo = shift_mix(l_raw, lastl_ref[0:1, :], mul_ref[...])
    last_ref[0:1, 0:w] = r_raw[rows - 1:rows]
    last_ref[0:1, w:2 * w] = k_raw[rows - 1:rows]
    last_ref[0:1, 2 * w:3 * w] = v_raw[rows - 1:rows]
    lastl_ref[0:1, :] = l_raw[rows - 1:rows]

    c0, c1 = LORA_PAD[0], LORA_PAD[0] + LORA_PAD[1]
    wd, ad, gd = lo[:, 0:c0], lo[:, c0:c1], lo[:, c1:]
    wlog = -_softplus(-(w0_ref[...] + _dot3(jnp.tanh(wd), w2h_ref[...], w2l_ref[...]))) - 0.5
    logd = -jnp.exp(wlog)
    a = _sigmoid(a0_ref[...] + _dot3(ad, a2h_ref[...], a2l_ref[...]))
    g_ref[...] = _dot3(_sigmoid(gd), g2h_ref[...], g2l_ref[...])

    kk = k * kk_ref[...]
    kk = kk * jnp.minimum(lax.rsqrt(_group_sum(kk * kk, RWKV_HEAD_DIM)), 1e12)
    k = k * (1.0 + (a - 1.0) * ka_ref[...])
    b = kk * a
    bonus_ref[...] = _group_sum(r * k * rk_ref[...], RWKV_HEAD_DIM) * v

    tr = lax.broadcasted_iota(jnp.int32, (rows, rows), 0)
    tc = lax.broadcasted_iota(jnp.int32, (rows, rows), 1)
    same = (tr // CHUNK) == (tc // CHUNK)
    cum, tot = _split_dot_lhs([(same & (tc <= tr)).astype(BF16), same.astype(BF16)], logd)

    e_prev = jnp.exp(cum - logd)
    e_cum = jnp.exp(cum)
    e_neg = jnp.exp(-cum)
    e_rem = jnp.exp(tot - cum)
    at_ref[...] = (-kk * e_prev).astype(BF16)
    rt_ref[...] = (r * e_cum).astype(BF16)
    bt_ref[...] = (b * e_neg).astype(BF16)
    kt_ref[...] = (k * e_neg).astype(BF16)
    vo_ref[...] = v.astype(BF16)
    bht_ref[...] = (b * e_rem).T.astype(BF16)
    kht_ref[...] = (k * e_rem).T.astype(BF16)
    gct_ref[...] = jnp.exp(tot).T


def _split_dot_lhs(lhs_list, b):
    accs = [None] * len(lhs_list)
    rem = b
    for _ in range(3):
        hi = rem.astype(BF16)
        for n, a in enumerate(lhs_list):
            term = jnp.dot(a, hi, preferred_element_type=F32)
            accs[n] = term if accs[n] is None else accs[n] + term
        rem = rem - hi.astype(F32)
    return accs


def _rwkv_pre(p, mu, mul, w0, w2, a0, a2, g2, k_k, k_a, r_k):
    tp = p.shape[0]
    w = RWKV_WIDTH
    rows = RWKV_ROWS
    first = 3 * DIFF_WIDTH // w
    lora_blk = (3 * DIFF_WIDTH + 3 * w) // LORA_COLS

    def full(arr):
        return pl.BlockSpec(arr.shape, lambda i: (0, 0))

    tok = pl.BlockSpec((rows, w), lambda i: (i, 0))
    chan = pl.BlockSpec((w, rows), lambda i: (0, i))
    tok_bf = jax.ShapeDtypeStruct((tp, w), BF16)
    params = (mu, mul, w0, *w2, a0, *a2, *g2, k_k, k_a, r_k)
    return pl.pallas_call(
        _rwkv_pre_body,
        grid=(tp // rows,),
        in_specs=[
            pl.BlockSpec((rows, w), lambda i: (i, first)),
            pl.BlockSpec((rows, w), lambda i: (i, first + 1)),
            pl.BlockSpec((rows, w), lambda i: (i, first + 2)),
            pl.BlockSpec((rows, LORA_COLS), lambda i: (i, lora_blk)),
            *[full(a) for a in params],
        ],
        out_specs=[tok, tok, tok, tok, tok, chan, chan, chan, tok, tok],
        out_shape=[tok_bf, tok_bf, tok_bf, tok_bf, tok_bf,
                   jax.ShapeDtypeStruct((w, tp), BF16), jax.ShapeDtypeStruct((w, tp), BF16),
                   jax.ShapeDtypeStruct((w, tp), F32),
                   jax.ShapeDtypeStruct((tp, w), F32), jax.ShapeDtypeStruct((tp, w), F32)],
        scratch_shapes=[pltpu.VMEM((8, 3 * w), F32), pltpu.VMEM((8, LORA_COLS), F32)],
        compiler_params=_cparams(("arbitrary",)),
        name="rwkv_pre",
    )(p, p, p, p, *params)


def _rwkv_chunk_body(at_ref, rt_ref, bt_ref, kt_ref, v_ref, bht_ref, kht_ref, gct_ref, g_ref, bonus_ref,
                     lnw_ref, lnb_ref, y_ref, state_ref, yn_ref):
    n = RWKV_HEAD_DIM
    c = CHUNK

    @pl.when(pl.program_id(0) == 0)
    def _():
        state_ref[...] = jnp.zeros_like(state_ref)

    tr = lax.broadcasted_iota(jnp.int32, (2 * c, 2 * c), 0)
    tc = lax.broadcasted_iota(jnp.int32, (2 * c, 2 * c), 1) % c
    mask = ((tr < c) & (tc < tr)) | ((tr >= c) & (tc <= tr - c))
    er = lax.broadcasted_iota(jnp.int32, (c, 2 * c), 0)
    ec = lax.broadcasted_iota(jnp.int32, (c, 2 * c), 1)
    right = ec >= c
    eye2 = (ec == er + c).astype(F32)
    zeros_cn = jnp.zeros((c, n), BF16)
    dot = functools.partial(jnp.dot, preferred_element_type=F32)
    bf = lambda t: t.astype(BF16)

    heads = range(RWKV_HEADS)
    chunks = range(at_ref.shape[0] // c)
    units = [(ci, h) for ci in chunks for h in heads]

    def blk(ref, u):
        return ref[u[0] * c:(u[0] + 1) * c, u[1] * n:(u[1] + 1) * n]

    lhs = {u: jnp.concatenate([blk(at_ref, u), blk(rt_ref, u)], axis=0) for u in units}
    w = {}
    for u in units:
        rhs = jnp.concatenate([blk(bt_ref, u), blk(kt_ref, u)], axis=0)
        wu = lax.dot_general(lhs[u], rhs, (((1,), (1,)), ((), ())), preferred_element_type=F32)
        w[u] = jnp.where(mask, wu, 0.0)
    top = {u: bf(w[u][:c]) for u in units}
    bot = {u: bf(w[u][c:]) for u in units}
    xs = {u: jnp.where(right, eye2, w[u][:c]) for u in units}
    for _ in range(int(math.log2(c))):
        xb = {u: bf(xs[u]) for u in units}
        xs = {u: dot(xb[u][:, :c], xb[u]) + jnp.where(right, xs[u], 0.0) for u in units}
    invb = {u: bf(xs[u]) for u in units}

    for ci in chunks:
        us = [(ci, h) for h in heads]
        h0 = [state_ref[h] for h in heads]
        x0 = [dot(lhs[u], bf(h0[u[1]])) for u in us]
        v = [blk(v_ref, u) for u in us]
        sys_rhs = [x0[h][:c] + dot(top[(ci, h)], jnp.concatenate([zeros_cn, v[h]], axis=0)) for h in heads]
        u_sol = [dot(invb[(ci, h)], jnp.concatenate([zeros_cn, bf(sys_rhs[h])], axis=0)) for h in heads]
        uv = [jnp.concatenate([bf(u_sol[h]), v[h]], axis=0) for h in heads]
        for h in heads:
            rs = slice(ci * c, (ci + 1) * c)
            cs = slice(h * n, (h + 1) * n)
            yn_ref[rs, cs] = x0[h][c:] + dot(bot[(ci, h)], uv[h])
            bkt = jnp.concatenate([bht_ref[cs, rs], kht_ref[cs, rs]], axis=1)
            state_ref[h] = gct_ref[cs, rs] * h0[h] + dot(bkt, uv[h])

    y = yn_ref[...]
    inv_n = 1.0 / n
    d = y - _group_sum(y, n) * inv_n
    var = _group_sum(d * d, n) * inv_n
    yn = d * lax.rsqrt(var + RWKV_LN_EPS)
    y_ref[...] = ((yn * lnw_ref[...] + lnb_ref[...] + bonus_ref[...]) * g_ref[...]).astype(BF16)


def _rwkv_chunks(at, rt, bt, kt, v, bht, kht, gct, g, bonus, lnw, lnb):
    tp, w = at.shape
    rows = RWKV_ROWS
    tok = pl.BlockSpec((rows, w), lambda i: (i, 0))
    chan = pl.BlockSpec((w, rows), lambda i: (0, i))
    vec = pl.BlockSpec((1, w), lambda i: (0, 0))
    return pl.pallas_call(
        _rwkv_chunk_body,
        grid=(tp // rows,),
        in_specs=[tok, tok, tok, tok, tok, chan, chan, chan, tok, tok, vec, vec],
        out_specs=tok,
        out_shape=jax.ShapeDtypeStruct((tp, w), BF16),
        scratch_shapes=[pltpu.VMEM((RWKV_HEADS, RWKV_HEAD_DIM, RWKV_HEAD_DIM), F32),
                        pltpu.VMEM((rows, w), F32)],
        compiler_params=_cparams(("arbitrary",)),
        name="rwkv_chunks",
    )(at, rt, bt, kt, v, bht, kht, gct, g, bonus, lnw, lnb)


def _attn_pre_body(q_ref, k_ref, v_ref, qg_ref, kg_ref, qz_ref, kn_ref, vt_ref):
    d = DIFF_QK_DIM

    def qk_norm(x, gain):
        ms = _group_sum(x * x, d) * (1.0 / d)
        return x * lax.rsqrt(ms + NORM_EPS) * gain

    q = qk_norm(q_ref[...], qg_ref[...]) * (d ** -0.5 * math.log2(math.e))
    first = (lax.broadcasted_iota(jnp.int32, q.shape, 1) % (2 * d)) < d
    qz_ref[0] = jnp.where(first, q, 0.0).astype(BF16)
    qz_ref[1] = jnp.where(first, 0.0, q).astype(BF16)
    kn_ref[...] = qk_norm(k_ref[...], kg_ref[...]).astype(BF16)
    vt = v_ref[...].T.astype(BF16)
    e = DIFF_V_DIM
    fill_row = lax.broadcasted_iota(jnp.int32, (VT_ROWS - e, vt.shape[1]), 0)
    fill = jnp.where(fill_row == 0, 1.0, 0.0).astype(BF16)
    for h in range(DIFF_HEADS):
        vt_ref[h * VT_ROWS:h * VT_ROWS + e, :] = vt[h * e:(h + 1) * e]
        vt_ref[h * VT_ROWS + e:(h + 1) * VT_ROWS, :] = fill


def _attn_pre(p, q_gain, k_gain):
    tp = p.shape[0]
    w = DIFF_WIDTH
    vec = pl.BlockSpec((1, w), lambda i: (0, 0))
    blk = lambda j: pl.BlockSpec((ROW_TILE, w), lambda i: (i, j))
    return pl.pallas_call(
        _attn_pre_body,
        grid=(tp // ROW_TILE,),
        in_specs=[blk(0), blk(1), blk(2), vec, vec],
        out_specs=[pl.BlockSpec((2, ROW_TILE, w), lambda i: (0, i, 0)), blk(0),
                   pl.BlockSpec((DIFF_HEADS * VT_ROWS, ROW_TILE), lambda i: (0, i))],
        out_shape=[jax.ShapeDtypeStruct((2, tp, w), BF16), jax.ShapeDtypeStruct((tp, w), BF16),
                   jax.ShapeDtypeStruct((DIFF_HEADS * VT_ROWS, tp), BF16)],
        compiler_params=_cparams(("arbitrary",)),
        name="attn_pre",
    )(p, p, p, q_gain, k_gain)


FLASH_HEADS = 4
VT_ROWS = DIFF_V_DIM + 16
FLASH_QCOLS = 256


def _flash_body(qi_ref, ki_ref, qz_ref, k_ref, vt_ref, lam_ref, sw_ref, o_ref, m_ref, acc_ref,
                *, first_key, lambda_init):
    step = pl.program_id(1)
    qi = qi_ref[step]
    ki = ki_ref[step]
    tq = o_ref.shape[0]
    tk = k_ref.shape[0]
    e = DIFF_V_DIM
    heads = range(FLASH_HEADS)
    lanes = [slice(g * e, (g + 1) * e) for g in heads]

    @pl.when(ki == 0)
    def _():
        m_ref[...] = jnp.full_like(m_ref, -jnp.inf)
        acc_ref[...] = jnp.zeros_like(acc_ref)

    def strand(g, c0, rows, masked):
        cols = slice(c0, c0 + FLASH_QCOLS)
        which, off = divmod(c0, tq)
        n = rows.stop - rows.start
        st = {}

        def stage_scores():
            q = qz_ref[which, off:off + FLASH_QCOLS, lanes[g]]
            s = lax.dot_general(k_ref[rows, lanes[g]], q, (((1,), (1,)), ((), ())),
                                preferred_element_type=F32)
            if masked:
                k_pos = ki * tk + rows.start + lax.broadcasted_iota(jnp.int32, (n, FLASH_QCOLS), 0)
                q_pos = qi * tq + off + lax.broadcasted_iota(jnp.int32, (n, FLASH_QCOLS), 1)
                vis = (k_pos <= q_pos) & ((k_pos >= first_key) | (q_pos < first_key))
                s = jnp.where(vis, s, -jnp.inf)
            st["s"] = s

        def stage_softmax():
            m_prev = m_ref[g, :, cols]
            m_new = jnp.maximum(m_prev, jnp.max(st["s"], axis=0, keepdims=True))
            st["alpha"] = jnp.exp2(m_prev - m_new)
            p = jnp.exp2(st["s"] - m_new)
            m_ref[g, :, cols] = m_new
            st["p"] = p.astype(BF16)

        def stage_values():
            acc_ref[g, :, cols] = st["alpha"] * acc_ref[g, :, cols] + jnp.dot(
                vt_ref[g * VT_ROWS:(g + 1) * VT_ROWS, rows], st["p"], preferred_element_type=F32)

        return [stage_scores, stage_softmax, stage_values]

    def run(rows_for, masked):
        strands = [strand(g, c0, rows_for(c0 % tq), masked)
                   for c0 in range(0, 2 * tq, FLASH_QCOLS) for g in heads]
        depth = len(strands[0])
        for t in range(len(strands) + depth - 1):
            for stage in range(depth):
                if 0 <= t - stage < len(strands):
                    strands[t - stage][stage]()

    everything = slice(0, tk)
    tail = slice(first_key // LANES * LANES, tk)

    @pl.when(ki == qi)
    def _():
        run(lambda off: slice(0, off + FLASH_QCOLS), True)

    @pl.when((ki == 0) & (qi > 0))
    def _():
        run(lambda off: tail, True)

    @pl.when((ki > 0) & (ki < qi))
    def _():
        run(lambda off: everything, False)

    @pl.when(ki == qi)
    def _():
        lp = lam_ref[...]
        lam = (jnp.exp(jnp.sum(lp[0:1] * lp[1:2])) - jnp.exp(jnp.sum(lp[2:3] * lp[3:4])) + lambda_init)
        for g in heads:
            o = acc_ref[g, :e] / acc_ref[g, e:e + 1]
            o = o[:, :tq] - lam * o[:, tq:]
            ms = jnp.mean(o * o, axis=0, keepdims=True)
            o = (o * lax.rsqrt(ms + SUBLN_EPS)).T * (sw_ref[...] * (1.0 - lambda_init))
            o_ref[:, lanes[g]] = o.astype(BF16)


def _flash(qz, kn, vt, lam_p, subln_w, first_key, lambda_init):
    _, tp, w = qz.shape
    e = DIFF_V_DIM
    we = FLASH_HEADS * e
    nb = tp // ROW_TILE
    pairs = [(qi, ki) for qi in range(nb) for ki in range(qi + 1)]
    qi_tab = jnp.asarray(np.array([p[0] for p in pairs], np.int32))
    ki_tab = jnp.asarray(np.array([p[1] for p in pairs], np.int32))
    grid_spec = pltpu.PrefetchScalarGridSpec(
        num_scalar_prefetch=2,
        grid=(DIFF_HEADS // FLASH_HEADS, len(pairs)),
        in_specs=[
            pl.BlockSpec((2, ROW_TILE, we), lambda h, s, qt, kt: (0, qt[s], h)),
            pl.BlockSpec((ROW_TILE, we), lambda h, s, qt, kt: (kt[s], h)),
            pl.BlockSpec((FLASH_HEADS * VT_ROWS, ROW_TILE), lambda h, s, qt, kt: (h, kt[s])),
            pl.BlockSpec(lam_p.shape, lambda h, s, qt, kt: (0, 0)),
            pl.BlockSpec((1, e), lambda h, s, qt, kt: (0, 0)),
        ],
        out_specs=pl.BlockSpec((ROW_TILE, we), lambda h, s, qt, kt: (qt[s], h)),
        scratch_shapes=[pltpu.VMEM((FLASH_HEADS, 1, 2 * ROW_TILE), F32),
                        pltpu.VMEM((FLASH_HEADS, VT_ROWS, 2 * ROW_TILE), F32)],
    )
    return pl.pallas_call(
        functools.partial(_flash_body, first_key=first_key, lambda_init=lambda_init),
        grid_spec=grid_spec,
        out_shape=jax.ShapeDtypeStruct((tp, w), BF16),
        compiler_params=_cparams(("arbitrary", "arbitrary")),
        name="diff_flash",
    )(qi_tab, ki_tab, qz, kn, vt, lam_p, subln_w)


def _out_proj_body(ya_ref, yb_ref, wa_ref, wb_ref, head_ref, x_ref, g_ref, h2_ref, hn_ref, *, head_blocks):
    acc = jnp.dot(ya_ref[...], wa_ref[...], preferred_element_type=F32)
    acc = acc + jnp.dot(yb_ref[...], wb_ref[...], preferred_element_type=F32)
    h2 = jnp.where(pl.program_id(0) < head_blocks, head_ref[...], x_ref[...]) + acc
    h2_ref[...] = h2
    ms = jnp.mean(h2 * h2, axis=-1, keepdims=True)
    hn_ref[...] = (h2 * lax.rsqrt(ms + NORM_EPS) * g_ref[...]).astype(BF16)


def _out_proj(ya, yb, wa, wb, head, x, gain, row_tile):
    d = x.shape[1]
    tp = head.shape[0] + x.shape[0]
    nhb = head.shape[0] // row_tile
    rows = pl.BlockSpec((row_tile, d), lambda i: (i, 0))
    return pl.pallas_call(
        functools.partial(_out_proj_body, head_blocks=nhb),
        grid=(tp // row_tile,),
        in_specs=[
            pl.BlockSpec((row_tile, ya.shape[1]), lambda i: (i, 0)),
            pl.BlockSpec((row_tile, yb.shape[1]), lambda i: (i, 0)),
            pl.BlockSpec(wa.shape, lambda i: (0, 0)),
            pl.BlockSpec(wb.shape, lambda i: (0, 0)),
            pl.BlockSpec((row_tile, d), lambda i: (jnp.minimum(i, nhb - 1), 0)),
            pl.BlockSpec((row_tile, d), lambda i: (jnp.maximum(i - nhb, 0), 0)),
            pl.BlockSpec((1, d), lambda i: (0, 0)),
        ],
        out_specs=[rows, rows],
        out_shape=[jax.ShapeDtypeStruct((tp, d), F32), jax.ShapeDtypeStruct((tp, d), BF16)],
        compiler_params=_cparams(("arbitrary",)),
        name="out_proj",
    )(ya, yb, wa, wb, head, x, gain)


FFN_SUB = 256


def _ffn_body(x_ref, wg_ref, wu_ref, cwg_ref, cwu_ref, cbg_ref, cbu_ref, wd_ref, h2_ref, o_ref,
              acc_ref, ua_ref, ub_ref, *, nf):
    s = pl.program_id(0)
    halo = x_ref.shape[0] - o_ref.shape[0]
    tile = jnp.maximum(s - 1, 0) % nf
    subs = [slice(c, c + FFN_SUB) for c in range(0, wg_ref.shape[1], FFN_SUB)]

    @pl.when(s == 0)
    def _():
        ua_ref[...] = jnp.zeros_like(ua_ref)
        ub_ref[...] = jnp.zeros_like(ub_ref)

    def conv(u, cw_ref, cb_ref, cols):
        cw = cw_ref[:, cols]
        u1 = pltpu.roll(u, 1, axis=0)
        y = pltpu.roll(u1, 1, axis=0) * cw[0:1] + u1 * cw[1:2] + u * cw[2:3]
        return y[halo:] + cb_ref[:, cols]

    def step(u_next, u_prev):
        x = x_ref[...]
        for c in subs:
            u_next[0, :, c] = jnp.dot(x, wg_ref[:, c], preferred_element_type=F32)
            u_next[1, :, c] = jnp.dot(x, wu_ref[:, c], preferred_element_type=F32)
        part = None
        for c in subs:
            gate = conv(u_prev[0, :, c], cwg_ref, cbg_ref, c)
            up = conv(u_prev[1, :, c], cwu_ref, cbu_ref, c)
            act = (gate * _sigmoid(gate) * up).astype(BF16)
            term = jnp.dot(act, wd_ref[c, :], preferred_element_type=F32)
            part = term if part is None else part + term
        acc_ref[...] = jnp.where(tile == 0, part, acc_ref[...] + part)

    @pl.when(s % 2 == 0)
    def _():
        step(ua_ref, ub_ref)

    @pl.when(s % 2 == 1)
    def _():
        step(ub_ref, ua_ref)

    @pl.when((s > 0) & (tile == nf - 1))
    def _():
        o_ref[...] = h2_ref[...] + acc_ref[...]


def _ffn(hn, h2, w_up, conv_w, conv_b, w_down, first_block, n_out_rows, ff_tile):
    tp, d = hn.shape
    dff = w_down.shape[0]
    nf = dff // ff_tile
    halo = 16
    fb = first_block
    items = (n_out_rows // ROW_TILE) * nf

    def up_item(s):
        it = jnp.minimum(s, items - 1)
        return it // nf, it % nf

    def down_item(s):
        it = jnp.maximum(s - 1, 0)
        return it // nf, it % nf

    return pl.pallas_call(
        functools.partial(_ffn_body, nf=nf),
        grid=(items + 1,),
        in_specs=[
            pl.BlockSpec((pl.Element(ROW_TILE + halo), pl.Element(d)),
                         lambda s: (pl.multiple_of((up_item(s)[0] + fb) * ROW_TILE - halo, halo), 0)),
            pl.BlockSpec((d, ff_tile), lambda s: (0, up_item(s)[1])),
            pl.BlockSpec((d, ff_tile), lambda s: (0, up_item(s)[1] + nf)),
            pl.BlockSpec((CONV_WIDTH, ff_tile), lambda s: (0, down_item(s)[1])),
            pl.BlockSpec((CONV_WIDTH, ff_tile), lambda s: (0, down_item(s)[1] + nf)),
            pl.BlockSpec((1, ff_tile), lambda s: (0, down_item(s)[1])),
            pl.BlockSpec((1, ff_tile), lambda s: (0, down_item(s)[1] + nf)),
            pl.BlockSpec((ff_tile, d), lambda s: (down_item(s)[1], 0)),
            pl.BlockSpec((ROW_TILE, d), lambda s: (down_item(s)[0] + fb, 0)),
        ],
        out_specs=pl.BlockSpec((ROW_TILE, d), lambda s: (down_item(s)[0], 0)),
        out_shape=jax.ShapeDtypeStruct((n_out_rows, d), F32),
        scratch_shapes=[pltpu.VMEM((ROW_TILE, d), F32),
                        pltpu.VMEM((2, ROW_TILE + halo, ff_tile), F32),
                        pltpu.VMEM((2, ROW_TILE + halo, ff_tile), F32)],
        compiler_params=_cparams(("arbitrary",)),
        name="conv_glu_ffn",
    )(hn, w_up, w_up, conv_w, conv_w, conv_b, conv_b, w_down, h2)


def _pad_cols(a, width):
    return jnp.pad(a, ((0, 0), (0, width - a.shape[1])))


def _layer(head, x, first_key, lambda_init, norm1_g, w_in, mu, w0, w2, a0, a2, g2, k_k, k_a, r_k, lnx_w, lnx_b,
           q_gain, k_gain, lam_q1, lam_k1, lam_q2, lam_k2, subln_w, w_out, norm2_g, w_up, conv_w, conv_b,
           w_down):
    w = RWKV_WIDTH
    row = lambda a: a.reshape(1, -1)
    n_rkv = 3 * w
    cuts = np.cumsum((n_rkv,) + LORA_SIZES)
    segs = [w_in[:, cuts[-1]:], w_in[:, :n_rkv]]
    segs += [_pad_cols(w_in[:, lo:hi], wd) for lo, hi, wd in zip(cuts[:-1], cuts[1:], LORA_PAD)]
    w_proj = _pad_cols(jnp.concatenate(segs, axis=1), PROJ_COLS).astype(BF16)
    mu_row = row(mu)
    mu_l = jnp.concatenate([_pad_cols(mu_row[:, lo:hi], wd) for lo, hi, wd in zip(cuts[:-1], cuts[1:], LORA_PAD)],
                           axis=1)

    def lora_weight(wt, rows):
        full = jnp.pad(wt, ((0, rows - wt.shape[0]), (0, 0)))
        hi = full.astype(BF16)
        return hi, (full - hi.astype(F32)).astype(BF16)

    p = _proj(head, x, row(norm1_g), w_proj, PROJ_COLS // 3)

    at, rt, bt, kt, v, bht, kht, gct, g, bonus = _rwkv_pre(
        p, mu_row[:, :n_rkv], mu_l, row(w0), lora_weight(w2, LORA_PAD[0]), row(a0),
        lora_weight(a2, LORA_PAD[1]), lora_weight(g2, LORA_PAD[2]), row(k_k), row(k_a), row(r_k))
    y_a = _rwkv_chunks(at, rt, bt, kt, v, bht, kht, gct, g, bonus, row(lnx_w), row(lnx_b))

    tile_heads = lambda a: jnp.tile(row(a), (1, 2 * DIFF_HEADS))
    qz, kn, vt = _attn_pre(p, tile_heads(q_gain), tile_heads(k_gain))
    lam_p = jnp.stack([lam_q1, lam_k1, lam_q2, lam_k2]).astype(F32)
    y_b = _flash(qz, kn, vt, lam_p, row(subln_w), first_key, lambda_init)

    w_out_b = w_out.astype(BF16)
    h2, hn2 = _out_proj(y_a, y_b, w_out_b[:w], w_out_b[w:], head, x, row(norm2_g), ROW_TILE)

    return _ffn(hn2, h2, w_up.astype(BF16), conv_w, row(conv_b), w_down.astype(BF16), 1, x.shape[0], 512)


def kernel(x, meta, norm1_g, w_in, mu, w0, w2, a0, a2, g2, k_k, k_a, r_k, lnx_w, lnx_b, q_gain, k_gain,
           lam_q1, lam_k1, lam_q2, lam_k2, subln_w, w_out, norm2_g, w_up, conv_w, conv_b, w_down):
    batch, seq, d = x.shape
    depth = norm1_g.shape[0]
    assert depth == 1 and seq % ROW_TILE == 0 and meta.shape[0] == N_META
    front = ROW_TILE - N_META
    head = jnp.concatenate([jnp.zeros((front, d), x.dtype), meta.astype(x.dtype)], axis=0)
    lambda_init = 0.8 - 0.6 * math.exp(-0.3 * 0)
    outs = [
        _layer(head, x[bi], front, lambda_init, norm1_g[0], w_in[0], mu[0], w0[0], w2[0], a0[0], a2[0], g2[0],
               k_k[0], k_a[0], r_k[0], lnx_w[0], lnx_b[0], q_gain[0], k_gain[0], lam_q1[0], lam_k1[0],
               lam_q2[0], lam_k2[0], subln_w[0], w_out[0], norm2_g[0], w_up[0], conv_w[0], conv_b[0], w_down[0])
        for bi in range(batch)]
    return outs[0][None] if batch == 1 else jnp.stack(outs, axis=0)
```

```python
import functools
import math

import jax
import jax.numpy as jnp
import numpy as np
from jax import lax
from jax.experimental import pallas as pl
from jax.experimental.pallas import tpu as pltpu

F32 = jnp.float32
BF16 = jnp.bfloat16

N_META = 16
RWKV_HEADS = 16
RWKV_HEAD_DIM = 64
RWKV_WIDTH = RWKV_HEADS * RWKV_HEAD_DIM
DECAY_LORA = 64
AAA_LORA = 64
GATE_LORA = 160
RWKV_LN_EPS = 64e-5
DIFF_HEADS = 8
DIFF_QK_DIM = 64
DIFF_V_DIM = 128
DIFF_WIDTH = DIFF_HEADS * DIFF_V_DIM
CONV_WIDTH = 3
NORM_EPS = 1e-6
SUBLN_EPS = 1e-5

LANES = 128
ROW_TILE = 512
CHUNK = 64
RWKV_ROWS = 4 * CHUNK
LORA_SIZES = (DECAY_LORA, AAA_LORA, GATE_LORA)
LORA_PAD = (128, 128, 256)
LORA_COLS = sum(LORA_PAD)
MXU_COLS = 256
PROJ_COLS = -(-(3 * DIFF_WIDTH + 3 * RWKV_WIDTH + LORA_COLS) // (3 * MXU_COLS)) * (3 * MXU_COLS)
VMEM_LIMIT = 56 * 1024 * 1024


def _cparams(sem):
    return pltpu.CompilerParams(dimension_semantics=sem, vmem_limit_bytes=VMEM_LIMIT)


def _split_dot(a, b_bf16, parts):
    acc = None
    rem = a
    for _ in range(parts):
        hi = rem.astype(BF16)
        term = jnp.dot(hi, b_bf16, preferred_element_type=F32)
        acc = term if acc is None else acc + term
        rem = rem - hi.astype(F32)
    return acc


def _dot3(a, b_hi, b_lo):
    ah = a.astype(BF16)
    al = (a - ah.astype(F32)).astype(BF16)
    d = functools.partial(jnp.dot, preferred_element_type=F32)
    return d(ah, b_hi) + (d(ah, b_lo) + d(al, b_hi))


def _group_ones(n, group):
    r = lax.broadcasted_iota(jnp.int32, (n, n), 0) // group
    c = lax.broadcasted_iota(jnp.int32, (n, n), 1) // group
    return (r == c).astype(BF16)


def _group_sum(x, group):
    ones = _group_ones(LANES, group)
    cols = [
        _split_dot(x[:, c:c + LANES], ones, 2)
        for c in range(0, x.shape[1], LANES)
    ]
    return jnp.concatenate(cols, axis=1)


def _proj_body(head_ref, x_ref, g_ref, w_ref, o_ref, hn_ref):
    @pl.when(pl.program_id(1) == 0)
    def _():
        x = jnp.where(pl.program_id(0) == 0, head_ref[...], x_ref[...])
        ms = jnp.mean(x * x, axis=-1, keepdims=True)
        hn_ref[...] = (x * lax.rsqrt(ms + NORM_EPS) * g_ref[...]).astype(BF16)

    o_ref[...] = jnp.dot(hn_ref[...], w_ref[...], preferred_element_type=F32)


def _proj(head, x, gain, w, col_tile):
    d = x.shape[1]
    tp = head.shape[0] + x.shape[0]
    n = w.shape[1]
    return pl.pallas_call(
        _proj_body,
        grid=(tp // ROW_TILE, n // col_tile),
        in_specs=[
            pl.BlockSpec((ROW_TILE, d), lambda i, j: (0, 0)),
            pl.BlockSpec((ROW_TILE, d), lambda i, j: (jnp.maximum(i - 1, 0), 0)),
            pl.BlockSpec((1, d), lambda i, j: (0, 0)),
            pl.BlockSpec((d, col_tile), lambda i, j: (0, j)),
        ],
        out_specs=pl.BlockSpec((ROW_TILE, col_tile), lambda i, j: (i, j)),
        out_shape=jax.ShapeDtypeStruct((tp, n), F32),
        scratch_shapes=[pltpu.VMEM((ROW_TILE, d), BF16)],
        compiler_params=_cparams(("arbitrary", "arbitrary")),
        name="proj",
    )(head, x, gain, w)


def _softplus(z):
    return jnp.maximum(z, 0.0) + jnp.log(1.0 + jnp.exp(-jnp.abs(z)))


def _sigmoid(z):
    return 0.5 * jnp.tanh(0.5 * z) + 0.5


def _rwkv_pre_body(r_ref, k_ref, v_ref, l_ref, mu_ref, mul_ref, w0_ref, w2h_ref, w2l_ref, a0_ref, a2h_ref,
                   a2l_ref, g2h_ref, g2l_ref, kk_ref, ka_ref, rk_ref,
                   at_ref, rt_ref, bt_ref, kt_ref, vo_ref, bht_ref, kht_ref, gct_ref, g_ref, bonus_ref,
                   last_ref, lastl_ref):
    i = pl.program_id(0)

    @pl.when(i == 0)
    def _():
        last_ref[...] = jnp.zeros_like(last_ref)
        lastl_ref[...] = jnp.zeros_like(lastl_ref)

    rows = r_ref.shape[0]
    row = lax.broadcasted_iota(jnp.int32, (rows, 1), 0)

    def shift_mix(p, last_row, mu):
        prev = jnp.where(row == 0, last_row, pltpu.roll(p, 1, axis=0))
        return p + (prev - p) * mu

    w = RWKV_WIDTH
    r_raw, k_raw, v_raw, l_raw = r_ref[...], k_ref[...], v_ref[...], l_ref[...]
    r = shift_mix(r_raw, last_ref[0:1, 0:w], mu_ref[:, 0:w])
    k = shift_mix(k_raw, last_ref[0:1, w:2 * w], mu_ref[:, w:2 * w])
    v = shift_mix(v_raw, last_ref[0:1, 2 * w:3 * w], mu_ref[:, 2 * w:3 * w])
    lo = shift_mix(l_raw, lastl_ref[0:1, :], mul_ref[...])
    last_ref[0:1, 0:w] = r_raw[rows - 1:rows]
    last_ref[0:1, w:2 * w] = k_raw[rows - 1:rows]
    last_ref[0:1, 2 * w:3 * w] = v_raw[rows - 1:rows]
    lastl_ref[0:1, :] = l_raw[rows - 1:rows]

    c0, c1 = LORA_PAD[0], LORA_PAD[0] + LORA_PAD[1]
    wd, ad, gd = lo[:, 0:c0], lo[:, c0:c1], lo[:, c1:]
    wlog = -_softplus(-(w0_ref[...] + _dot3(jnp.tanh(wd), w2h_ref[...], w2l_ref[...]))) - 0.5
    logd = -jnp.exp(wlog)
    a = _sigmoid(a0_ref[...] + _dot3(ad, a2h_ref[...], a2l_ref[...]))
    g_ref[...] = _dot3(_sigmoid(gd), g2h_ref[...], g2l_ref[...])

    kk = k * kk_ref[...]
    kk = kk * jnp.minimum(lax.rsqrt(_group_sum(kk * kk, RWKV_HEAD_DIM)), 1e12)
    k = k * (1.0 + (a - 1.0) * ka_ref[...])
    b = kk * a
    bonus_ref[...] = _group_sum(r * k * rk_ref[...], RWKV_HEAD_DIM) * v

    tr = lax.broadcasted_iota(jnp.int32, (rows, rows), 0)
    tc = lax.broadcasted_iota(jnp.int32, (rows, rows), 1)
    same = (tr // CHUNK) == (tc // CHUNK)
    cum, tot = _split_dot_lhs([(same & (tc <= tr)).astype(BF16), same.astype(BF16)], logd)

    e_prev = jnp.exp(cum - logd)
    e_cum = jnp.exp(cum)
    e_neg = jnp.exp(-cum)
    e_rem = jnp.exp(tot - cum)
    at_ref[...] = (-kk * e_prev).astype(BF16)
    rt_ref[...] = (r * e_cum).astype(BF16)
    bt_ref[...] = (b * e_neg).astype(BF16)
    kt_ref[...] = (k * e_neg).astype(BF16)
    vo_ref[...] = v.astype(BF16)
    bht_ref[...] = (b * e_rem).T.astype(BF16)
    kht_ref[...] = (k * e_rem).T.astype(BF16)
    gct_ref[...] = jnp.exp(tot).T


def _split_dot_lhs(lhs_list, b):
    accs = [None] * len(lhs_list)
    rem = b
    for _ in range(3):
        hi = rem.astype(BF16)
        for n, a in enumerate(lhs_list):
            term = jnp.dot(a, hi, preferred_element_type=F32)
            accs[n] = term if accs[n] is None else accs[n] + term
        rem = rem - hi.astype(F32)
    return accs


def _rwkv_pre(p, mu, mul, w0, w2, a0, a2, g2, k_k, k_a, r_k):
    tp = p.shape[0]
    w = RWKV_WIDTH
    rows = RWKV_ROWS
    first = 3 * DIFF_WIDTH // w
    lora_blk = (3 * DIFF_WIDTH + 3 * w) // LORA_COLS

    def full(arr):
        return pl.BlockSpec(arr.shape, lambda i: (0, 0))

    tok = pl.BlockSpec((rows, w), lambda i: (i, 0))
    chan = pl.BlockSpec((w, rows), lambda i: (0, i))
    tok_bf = jax.ShapeDtypeStruct((tp, w), BF16)
    params = (mu, mul, w0, *w2, a0, *a2, *g2, k_k, k_a, r_k)
    return pl.pallas_call(
        _rwkv_pre_body,
        grid=(tp // rows,),
        in_specs=[
            pl.BlockSpec((rows, w), lambda i: (i, first)),
            pl.BlockSpec((rows, w), lambda i: (i, first + 1)),
            pl.BlockSpec((rows, w), lambda i: (i, first + 2)),
            pl.BlockSpec((rows, LORA_COLS), lambda i: (i, lora_blk)),
            *[full(a) for a in params],
        ],
        out_specs=[tok, tok, tok, tok, tok, chan, chan, chan, tok, tok],
        out_shape=[tok_bf, tok_bf, tok_bf, tok_bf, tok_bf,
                   jax.ShapeDtypeStruct((w, tp), BF16), jax.ShapeDtypeStruct((w, tp), BF16),
                   jax.ShapeDtypeStruct((w, tp), F32),
                   jax.ShapeDtypeStruct((tp, w), F32), jax.ShapeDtypeStruct((tp, w), F32)],
        scratch_shapes=[pltpu.VMEM((8, 3 * w), F32), pltpu.VMEM((8, LORA_COLS), F32)],
        compiler_params=_cparams(("arbitrary",)),
        name="rwkv_pre",
    )(p, p, p, p, *params)


def _rwkv_chunk_body(at_ref, rt_ref, bt_ref, kt_ref, v_ref, bht_ref, kht_ref, gct_ref, g_ref, bonus_ref,
                     lnw_ref, lnb_ref, y_ref, state_ref, yn_ref):
    n = RWKV_HEAD_DIM
    c = CHUNK

    @pl.when(pl.program_id(0) == 0)
    def _():
        state_ref[...] = jnp.zeros_like(state_ref)

    tr = lax.broadcasted_iota(jnp.int32, (2 * c, 2 * c), 0)
    tc = lax.broadcasted_iota(jnp.int32, (2 * c, 2 * c), 1) % c
    mask = ((tr < c) & (tc < tr)) | ((tr >= c) & (tc <= tr - c))
    er = lax.broadcasted_iota(jnp.int32, (c, 2 * c), 0)
    ec = lax.broadcasted_iota(jnp.int32, (c, 2 * c), 1)
    right = ec >= c
    eye2 = (ec == er + c).astype(F32)
    zeros_cn = jnp.zeros((c, n), BF16)
    dot = functools.partial(jnp.dot, preferred_element_type=F32)
    bf = lambda t: t.astype(BF16)

    heads = range(RWKV_HEADS)
    chunks = range(at_ref.shape[0] // c)
    units = [(ci, h) for ci in chunks for h in heads]

    def blk(ref, u):
        return ref[u[0] * c:(u[0] + 1) * c, u[1] * n:(u[1] + 1) * n]

    lhs = {u: jnp.concatenate([blk(at_ref, u), blk(rt_ref, u)], axis=0) for u in units}
    w = {}
    for u in units:
        rhs = jnp.concatenate([blk(bt_ref, u), blk(kt_ref, u)], axis=0)
        wu = lax.dot_general(lhs[u], rhs, (((1,), (1,)), ((), ())), preferred_element_type=F32)
        w[u] = jnp.where(mask, wu, 0.0)
    top = {u: bf(w[u][:c]) for u in units}
    bot = {u: bf(w[u][c:]) for u in units}
    xs = {u: jnp.where(right, eye2, w[u][:c]) for u in units}
    for _ in range(int(math.log2(c))):
        xb = {u: bf(xs[u]) for u in units}
        xs = {u: dot(xb[u][:, :c], xb[u]) + jnp.where(right, xs[u], 0.0) for u in units}
    invb = {u: bf(xs[u]) for u in units}

    for ci in chunks:
        us = [(ci, h) for h in heads]
        h0 = [state_ref[h] for h in heads]
        x0 = [dot(lhs[u], bf(h0[u[1]])) for u in us]
        v = [blk(v_ref, u) for u in us]
        sys_rhs = [x0[h][:c] + dot(top[(ci, h)], jnp.concatenate([zeros_cn, v[h]], axis=0)) for h in heads]
        u_sol = [dot(invb[(ci, h)], jnp.concatenate([zeros_cn, bf(sys_rhs[h])], axis=0)) for h in heads]
        uv = [jnp.concatenate([bf(u_sol[h]), v[h]], axis=0) for h in heads]
        for h in heads:
            rs = slice(ci * c, (ci + 1) * c)
            cs = slice(h * n, (h + 1) * n)
            yn_ref[rs, cs] = x0[h][c:] + dot(bot[(ci, h)], uv[h])
            bkt = jnp.concatenate([bht_ref[cs, rs], kht_ref[cs, rs]], axis=1)
            state_ref[h] = gct_ref[cs, rs] * h0[h] + dot(bkt, uv[h])

    y = yn_ref[...]
    inv_n = 1.0 / n
    d = y - _group_sum(y, n) * inv_n
    var = _group_sum(d * d, n) * inv_n
    yn = d * lax.rsqrt(var + RWKV_LN_EPS)
    y_ref[...] = ((yn * lnw_ref[...] + lnb_ref[...] + bonus_ref[...]) * g_ref[...]).astype(BF16)


def _rwkv_chunks(at, rt, bt, kt, v, bht, kht, gct, g, bonus, lnw, lnb):
    tp, w = at.shape
    rows = RWKV_ROWS
    tok = pl.BlockSpec((rows, w), lambda i: (i, 0))
    chan = pl.BlockSpec((w, rows), lambda i: (0, i))
    vec = pl.BlockSpec((1, w), lambda i: (0, 0))
    return pl.pallas_call(
        _rwkv_chunk_body,
        grid=(tp // rows,),
        in_specs=[tok, tok, tok, tok, tok, chan, chan, chan, tok, tok, vec, vec],
        out_specs=tok,
        out_shape=jax.ShapeDtypeStruct((tp, w), BF16),
        scratch_shapes=[pltpu.VMEM((RWKV_HEADS, RWKV_HEAD_DIM, RWKV_HEAD_DIM), F32),
                        pltpu.VMEM((rows, w), F32)],
        compiler_params=_cparams(("arbitrary",)),
        name="rwkv_chunks",
    )(at, rt, bt, kt, v, bht, kht, gct, g, bonus, lnw, lnb)


def _attn_pre_body(q_ref, k_ref, v_ref, qg_ref, kg_ref, qz_ref, kn_ref, vt_ref):
    d = DIFF_QK_DIM

    def qk_norm(x, gain):
        ms = _group_sum(x * x, d) * (1.0 / d)
        return x * lax.rsqrt(ms + NORM_EPS) * gain

    q = qk_norm(q_ref[...], qg_ref[...]) * (d ** -0.5 * math.log2(math.e))
    first = (lax.broadcasted_iota(jnp.int32, q.shape, 1) % (2 * d)) < d
    qz_ref[0] = jnp.where(first, q, 0.0).astype(BF16)
    qz_ref[1] = jnp.where(first, 0.0, q).astype(BF16)
    kn_ref[...] = qk_norm(k_ref[...], kg_ref[...]).astype(BF16)
    vt = v_ref[...].T.astype(BF16)
    e = DIFF_V_DIM
    fill_row = lax.broadcasted_iota(jnp.int32, (VT_ROWS - e, vt.shape[1]), 0)
    fill = jnp.where(fill_row == 0, 1.0, 0.0).astype(BF16)
    for h in range(DIFF_HEADS):
        vt_ref[h * VT_ROWS:h * VT_ROWS + e, :] = vt[h * e:(h + 1) * e]
        vt_ref[h * VT_ROWS + e:(h + 1) * VT_ROWS, :] = fill


def _attn_pre(p, q_gain, k_gain):
    tp = p.shape[0]
    w = DIFF_WIDTH
    vec = pl.BlockSpec((1, w), lambda i: (0, 0))
    blk = lambda j: pl.BlockSpec((ROW_TILE, w), lambda i: (i, j))
    return pl.pallas_call(
        _attn_pre_body,
        grid=(tp // ROW_TILE,),
        in_specs=[blk(0), blk(1), blk(2), vec, vec],
        out_specs=[pl.BlockSpec((2, ROW_TILE, w), lambda i: (0, i, 0)), blk(0),
                   pl.BlockSpec((DIFF_HEADS * VT_ROWS, ROW_TILE), lambda i: (0, i))],
        out_shape=[jax.ShapeDtypeStruct((2, tp, w), BF16), jax.ShapeDtypeStruct((tp, w), BF16),
                   jax.ShapeDtypeStruct((DIFF_HEADS * VT_ROWS, tp), BF16)],
        compiler_params=_cparams(("arbitrary",)),
        name="attn_pre",
    )(p, p, p, q_gain, k_gain)


FLASH_HEADS = 4
VT_ROWS = DIFF_V_DIM + 16
FLASH_QCOLS = 256


def _flash_body(qi_ref, ki_ref, path_ref, qz_ref, k_ref, vt_ref, lam_ref, sw_ref, o_ref, m_ref, acc_ref,
                *, first_key, lambda_init):
    step = pl.program_id(1)
    qi = qi_ref[step]
    ki = ki_ref[step]
    path = path_ref[step]
    tq = o_ref.shape[0]
    tk = k_ref.shape[0]
    e = DIFF_V_DIM
    heads = range(FLASH_HEADS)
    lanes = [slice(g * e, (g + 1) * e) for g in heads]

    @pl.when(ki == 0)
    def _():
        m_ref[...] = jnp.full_like(m_ref, -jnp.inf)
        acc_ref[...] = jnp.zeros_like(acc_ref)

    def strand(g, c0, rows, mask_rows):
        cols = slice(c0, c0 + FLASH_QCOLS)
        which, off = divmod(c0, tq)
        st = {}

        def stage_scores():
            q = qz_ref[which, off:off + FLASH_QCOLS, lanes[g]]
            s = lax.dot_general(k_ref[rows, lanes[g]], q, (((1,), (1,)), ((), ())),
                                preferred_element_type=F32)
            if mask_rows is not None:
                n = mask_rows.stop - mask_rows.start
                k_pos = ki * tk + mask_rows.start + lax.broadcasted_iota(jnp.int32, (n, FLASH_QCOLS), 0)
                q_pos = qi * tq + off + lax.broadcasted_iota(jnp.int32, (n, FLASH_QCOLS), 1)
                vis = (k_pos <= q_pos) & ((k_pos >= first_key) | (q_pos < first_key))
                lo, hi = mask_rows.start - rows.start, mask_rows.stop - rows.start
                parts = [s[:lo], jnp.where(vis, s[lo:hi], -jnp.inf), s[hi:]]
                s = jnp.concatenate([part for part in parts if part.shape[0]], axis=0)
            st["s"] = s

        def stage_softmax():
            m_prev = m_ref[g, :, cols]
            m_new = jnp.maximum(m_prev, jnp.max(st["s"], axis=0, keepdims=True))
            st["alpha"] = jnp.exp2(m_prev - m_new)
            p = jnp.exp2(st["s"] - m_new)
            m_ref[g, :, cols] = m_new
            st["p"] = p.astype(BF16)

        def stage_values():
            acc_ref[g, :, cols] = st["alpha"] * acc_ref[g, :, cols] + jnp.dot(
                vt_ref[g * VT_ROWS:(g + 1) * VT_ROWS, rows], st["p"], preferred_element_type=F32)

        return [stage_scores, stage_softmax, stage_values]

    def run(rows_for):
        strands = [strand(g, c0, *rows_for(c0 % tq))
                   for c0 in range(0, 2 * tq, FLASH_QCOLS) for g in heads]
        depth = len(strands[0])
        for t in range(len(strands) + depth - 1):
            for stage in range(depth):
                if 0 <= t - stage < len(strands):
                    strands[t - stage][stage]()

    tail = first_key // LANES * LANES
    paths = {
        _PATH_DIAG_FIRST_HALF: lambda off: (slice(0, off + FLASH_QCOLS), slice(0, off + FLASH_QCOLS)),
        _PATH_DIAG_SECOND_HALF: lambda off: (slice(0, tq + off + FLASH_QCOLS), slice(tq, tq + off + FLASH_QCOLS)),
        _PATH_PAD_AND_DIAG: lambda off: (slice(tail, tq + off + FLASH_QCOLS), slice(tail, tq + off + FLASH_QCOLS)),
        _PATH_PAD: lambda off: (slice(tail, tk), slice(tail, tq)),
        _PATH_FULL: lambda off: (slice(0, tk), None),
    }
    for pid, rows_for in paths.items():
        pl.when(path == pid)(functools.partial(run, rows_for))

    @pl.when(path <= _PATH_PAD_AND_DIAG)
    def _():
        lp = lam_ref[...]
        lam = (jnp.exp(jnp.sum(lp[0:1] * lp[1:2])) - jnp.exp(jnp.sum(lp[2:3] * lp[3:4])) + lambda_init)
        for g in heads:
            o = acc_ref[g, :e] / acc_ref[g, e:e + 1]
            o = o[:, :tq] - lam * o[:, tq:]
            ms = jnp.mean(o * o, axis=0, keepdims=True)
            o = (o * lax.rsqrt(ms + SUBLN_EPS)).T * (sw_ref[...] * (1.0 - lambda_init))
            o_ref[:, lanes[g]] = o.astype(BF16)


_PATH_DIAG_FIRST_HALF = 0
_PATH_DIAG_SECOND_HALF = 1
_PATH_PAD_AND_DIAG = 2
_PATH_PAD = 3
_PATH_FULL = 4


def _flash(qz, kn, vt, lam_p, subln_w, first_key, lambda_init):
    _, tp, w = qz.shape
    e = DIFF_V_DIM
    we = FLASH_HEADS * e
    tk = 2 * ROW_TILE
    assert first_key < ROW_TILE
    steps = []
    for qi in range(tp // ROW_TILE):
        last = qi // 2
        for ki in range(last + 1):
            if ki == last:
                path = (_PATH_DIAG_FIRST_HALF if qi % 2 == 0 else
                        _PATH_PAD_AND_DIAG if ki == 0 else _PATH_DIAG_SECOND_HALF)
            else:
                path = _PATH_PAD if ki == 0 else _PATH_FULL
            steps.append((qi, ki, path))
    tabs = [jnp.asarray(np.array([s[n] for s in steps], np.int32)) for n in range(3)]
    grid_spec = pltpu.PrefetchScalarGridSpec(
        num_scalar_prefetch=3,
        grid=(DIFF_HEADS // FLASH_HEADS, len(steps)),
        in_specs=[
            pl.BlockSpec((2, ROW_TILE, we), lambda h, s, qt, kt, pt: (0, qt[s], h)),
            pl.BlockSpec((tk, we), lambda h, s, qt, kt, pt: (kt[s], h)),
            pl.BlockSpec((FLASH_HEADS * VT_ROWS, tk), lambda h, s, qt, kt, pt: (h, kt[s])),
            pl.BlockSpec(lam_p.shape, lambda h, s, qt, kt, pt: (0, 0)),
            pl.BlockSpec((1, e), lambda h, s, qt, kt, pt: (0, 0)),
        ],
        out_specs=pl.BlockSpec((ROW_TILE, we), lambda h, s, qt, kt, pt: (qt[s], h)),
        scratch_shapes=[pltpu.VMEM((FLASH_HEADS, 1, 2 * ROW_TILE), F32),
                        pltpu.VMEM((FLASH_HEADS, VT_ROWS, 2 * ROW_TILE), F32)],
    )
    return pl.pallas_call(
        functools.partial(_flash_body, first_key=first_key, lambda_init=lambda_init),
        grid_spec=grid_spec,
        out_shape=jax.ShapeDtypeStruct((tp, w), BF16),
        compiler_params=_cparams(("arbitrary", "arbitrary")),
        name="diff_flash",
    )(*tabs, qz, kn, vt, lam_p, subln_w)


def _out_proj_body(ya_ref, yb_ref, wa_ref, wb_ref, head_ref, x_ref, g_ref, h2_ref, hn_ref, *, head_blocks):
    acc = jnp.dot(ya_ref[...], wa_ref[...], preferred_element_type=F32)
    acc = acc + jnp.dot(yb_ref[...], wb_ref[...], preferred_element_type=F32)
    h2 = jnp.where(pl.program_id(0) < head_blocks, head_ref[...], x_ref[...]) + acc
    h2_ref[...] = h2
    ms = jnp.mean(h2 * h2, axis=-1, keepdims=True)
    hn_ref[...] = (h2 * lax.rsqrt(ms + NORM_EPS) * g_ref[...]).astype(BF16)


def _out_proj(ya, yb, wa, wb, head, x, gain, row_tile):
    d = x.shape[1]
    tp = head.shape[0] + x.shape[0]
    nhb = head.shape[0] // row_tile
    rows = pl.BlockSpec((row_tile, d), lambda i: (i, 0))
    return pl.pallas_call(
        functools.partial(_out_proj_body, head_blocks=nhb),
        grid=(tp // row_tile,),
        in_specs=[
            pl.BlockSpec((row_tile, ya.shape[1]), lambda i: (i, 0)),
            pl.BlockSpec((row_tile, yb.shape[1]), lambda i: (i, 0)),
            pl.BlockSpec(wa.shape, lambda i: (0, 0)),
            pl.BlockSpec(wb.shape, lambda i: (0, 0)),
            pl.BlockSpec((row_tile, d), lambda i: (jnp.minimum(i, nhb - 1), 0)),
            pl.BlockSpec((row_tile, d), lambda i: (jnp.maximum(i - nhb, 0), 0)),
            pl.BlockSpec((1, d), lambda i: (0, 0)),
        ],
        out_specs=[rows, rows],
        out_shape=[jax.ShapeDtypeStruct((tp, d), F32), jax.ShapeDtypeStruct((tp, d), BF16)],
        compiler_params=_cparams(("arbitrary",)),
        name="out_proj",
    )(ya, yb, wa, wb, head, x, gain)


FFN_SUB = 256


def _ffn_body(x_ref, wg_ref, wu_ref, cwg_ref, cwu_ref, cbg_ref, cbu_ref, wd_ref, h2_ref, o_ref,
              acc_ref, ua_ref, ub_ref, *, nf):
    s = pl.program_id(0)
    halo = x_ref.shape[0] - o_ref.shape[0]
    tile = jnp.maximum(s - 1, 0) % nf
    subs = [slice(c, c + FFN_SUB) for c in range(0, wg_ref.shape[1], FFN_SUB)]

    @pl.when(s == 0)
    def _():
        ua_ref[...] = jnp.zeros_like(ua_ref)
        ub_ref[...] = jnp.zeros_like(ub_ref)

    def conv(u, cw_ref, cb_ref, cols):
        cw = cw_ref[:, cols]
        u1 = pltpu.roll(u, 1, axis=0)
        y = pltpu.roll(u1, 1, axis=0) * cw[0:1] + u1 * cw[1:2] + u * cw[2:3]
        return y[halo:] + cb_ref[:, cols]

    def step(u_next, u_prev):
        x = x_ref[...]
        for c in subs:
            u_next[0, :, c] = jnp.dot(x, wg_ref[:, c], preferred_element_type=F32)
            u_next[1, :, c] = jnp.dot(x, wu_ref[:, c], preferred_element_type=F32)
        part = None
        for c in subs:
            gate = conv(u_prev[0, :, c], cwg_ref, cbg_ref, c)
            up = conv(u_prev[1, :, c], cwu_ref, cbu_ref, c)
            act = (gate * _sigmoid(gate) * up).astype(BF16)
            term = jnp.dot(act, wd_ref[c, :], preferred_element_type=F32)
            part = term if part is None else part + term
        acc_ref[...] = jnp.where(tile == 0, part, acc_ref[...] + part)

    @pl.when(s % 2 == 0)
    def _():
        step(ua_ref, ub_ref)

    @pl.when(s % 2 == 1)
    def _():
        step(ub_ref, ua_ref)

    @pl.when((s > 0) & (tile == nf - 1))
    def _():
        o_ref[...] = h2_ref[...] + acc_ref[...]


def _ffn(hn, h2, w_up, conv_w, conv_b, w_down, first_block, n_out_rows, ff_tile):
    tp, d = hn.shape
    dff = w_down.shape[0]
    nf = dff // ff_tile
    halo = 16
    fb = first_block
    items = (n_out_rows // ROW_TILE) * nf

    def up_item(s):
        it = jnp.minimum(s, items - 1)
        return it // nf, it % nf

    def down_item(s):
        it = jnp.maximum(s - 1, 0)
        return it // nf, it % nf

    return pl.pallas_call(
        functools.partial(_ffn_body, nf=nf),
        grid=(items + 1,),
        in_specs=[
            pl.BlockSpec((pl.Element(ROW_TILE + halo), pl.Element(d)),
                         lambda s: (pl.multiple_of((up_item(s)[0] + fb) * ROW_TILE - halo, halo), 0)),
            pl.BlockSpec((d, ff_tile), lambda s: (0, up_item(s)[1])),
            pl.BlockSpec((d, ff_tile), lambda s: (0, up_item(s)[1] + nf)),
            pl.BlockSpec((CONV_WIDTH, ff_tile), lambda s: (0, down_item(s)[1])),
            pl.BlockSpec((CONV_WIDTH, ff_tile), lambda s: (0, down_item(s)[1] + nf)),
            pl.BlockSpec((1, ff_tile), lambda s: (0, down_item(s)[1])),
            pl.BlockSpec((1, ff_tile), lambda s: (0, down_item(s)[1] + nf)),
            pl.BlockSpec((ff_tile, d), lambda s: (down_item(s)[1], 0)),
            pl.BlockSpec((ROW_TILE, d), lambda s: (down_item(s)[0] + fb, 0)),
        ],
        out_specs=pl.BlockSpec((ROW_TILE, d), lambda s: (down_item(s)[0], 0)),
        out_shape=jax.ShapeDtypeStruct((n_out_rows, d), F32),
        scratch_shapes=[pltpu.VMEM((ROW_TILE, d), F32),
                        pltpu.VMEM((2, ROW_TILE + halo, ff_tile), F32),
                        pltpu.VMEM((2, ROW_TILE + halo, ff_tile), F32)],
        compiler_params=_cparams(("arbitrary",)),
        name="conv_glu_ffn",
    )(hn, w_up, w_up, conv_w, conv_w, conv_b, conv_b, w_down, h2)


def _pad_cols(a, width):
    return jnp.pad(a, ((0, 0), (0, width - a.shape[1])))


def _layer(head, x, first_key, lambda_init, norm1_g, w_in, mu, w0, w2, a0, a2, g2, k_k, k_a, r_k, lnx_w, lnx_b,
           q_gain, k_gain, lam_q1, lam_k1, lam_q2, lam_k2, subln_w, w_out, norm2_g, w_up, conv_w, conv_b,
           w_down):
    w = RWKV_WIDTH
    row = lambda a: a.reshape(1, -1)
    n_rkv = 3 * w
    cuts = np.cumsum((n_rkv,) + LORA_SIZES)
    segs = [w_in[:, cuts[-1]:], w_in[:, :n_rkv]]
    segs += [_pad_cols(w_in[:, lo:hi], wd) for lo, hi, wd in zip(cuts[:-1], cuts[1:], LORA_PAD)]
    segs.append(jnp.zeros((w_in.shape[0], PROJ_COLS - sum(s.shape[1] for s in segs)), w_in.dtype))
    w_proj = jnp.concatenate(segs, axis=1).astype(BF16)
    mu_row = row(mu)
    mu_l = jnp.concatenate([_pad_cols(mu_row[:, lo:hi], wd) for lo, hi, wd in zip(cuts[:-1], cuts[1:], LORA_PAD)],
                           axis=1)

    def lora_weight(wt, rows):
        full = jnp.pad(wt, ((0, rows - wt.shape[0]), (0, 0)))
        hi = full.astype(BF16)
        return hi, (full - hi.astype(F32)).astype(BF16)

    p = _proj(head, x, row(norm1_g), w_proj, PROJ_COLS // 3)

    at, rt, bt, kt, v, bht, kht, gct, g, bonus = _rwkv_pre(
        p, mu_row[:, :n_rkv], mu_l, row(w0), lora_weight(w2, LORA_PAD[0]), row(a0),
        lora_weight(a2, LORA_PAD[1]), lora_weight(g2, LORA_PAD[2]), row(k_k), row(k_a), row(r_k))
    y_a = _rwkv_chunks(at, rt, bt, kt, v, bht, kht, gct, g, bonus, row(lnx_w), row(lnx_b))

    tile_heads = lambda a: jnp.tile(row(a), (1, 2 * DIFF_HEADS))
    qz, kn, vt = _attn_pre(p, tile_heads(q_gain), tile_heads(k_gain))
    lam_p = jnp.stack([lam_q1, lam_k1, lam_q2, lam_k2]).astype(F32)
    y_b = _flash(qz, kn, vt, lam_p, row(subln_w), first_key, lambda_init)

    w_out_b = w_out.astype(BF16)
    h2, hn2 = _out_proj(y_a, y_b, w_out_b[:w], w_out_b[w:], head, x, row(norm2_g), ROW_TILE)

    return _ffn(hn2, h2, w_up.astype(BF16), conv_w, row(conv_b), w_down.astype(BF16), 1, x.shape[0], 512)


def kernel(x, meta, norm1_g, w_in, mu, w0, w2, a0, a2, g2, k_k, k_a, r_k, lnx_w, lnx_b, q_gain, k_gain,
           lam_q1, lam_k1, lam_q2, lam_k2, subln_w, w_out, norm2_g, w_up, conv_w, conv_b, w_down):
    batch, seq, d = x.shape
    depth = norm1_g.shape[0]
    assert depth == 1 and seq % ROW_TILE == 0 and meta.shape[0] == N_META
    front = ROW_TILE - N_META
    head = jnp.concatenate([jnp.zeros((front, d), x.dtype), meta.astype(x.dtype)], axis=0)
    lambda_init = 0.8 - 0.6 * math.exp(-0.3 * 0)
    outs = [
        _layer(head, x[bi], front, lambda_init, norm1_g[0], w_in[0], mu[0], w0[0], w2[0], a0[0], a2[0], g2[0],
               k_k[0], k_a[0], r_k[0], lnx_w[0], lnx_b[0], q_gain[0], k_gain[0], lam_q1[0], lam_k1[0],
               lam_q2[0], lam_k2[0], subln_w[0], w_out[0], norm2_g[0], w_up[0], conv_w[0], conv_b[0], w_down[0])
        for bi in range(batch)]
    return outs[0][None] if batch == 1 else jnp.stack(outs, axis=0)
```

```python
import functools
import math

import jax
import jax.numpy as jnp
import numpy as np
from jax import lax
from jax.experimental import pallas as pl
from jax.experimental.pallas import tpu as pltpu

F32 = jnp.float32
BF16 = jnp.bfloat16

N_META = 16
RWKV_HEADS = 16
RWKV_HEAD_DIM = 64
RWKV_WIDTH = RWKV_HEADS * RWKV_HEAD_DIM
DECAY_LORA = 64
AAA_LORA = 64
GATE_LORA = 160
RWKV_LN_EPS = 64e-5
DIFF_HEADS = 8
DIFF_QK_DIM = 64
DIFF_V_DIM = 128
DIFF_WIDTH = DIFF_HEADS * DIFF_V_DIM
CONV_WIDTH = 3
NORM_EPS = 1e-6
SUBLN_EPS = 1e-5

LANES = 128
ROW_TILE = 512
CHUNK = 64
RWKV_ROWS = 4 * CHUNK
LORA_SIZES = (DECAY_LORA, AAA_LORA, GATE_LORA)
LORA_PAD = (128, 128, 256)
LORA_COLS = sum(LORA_PAD)
MXU_COLS = 256
PROJ_COLS = -(-(3 * DIFF_WIDTH + 3 * RWKV_WIDTH + LORA_COLS) // (3 * MXU_COLS)) * (3 * MXU_COLS)
VMEM_LIMIT = 56 * 1024 * 1024


def _cparams(sem):
    return pltpu.CompilerParams(dimension_semantics=sem, vmem_limit_bytes=VMEM_LIMIT)


def _split_dot(a, b_bf16, parts):
    acc = None
    rem = a
    for _ in range(parts):
        hi = rem.astype(BF16)
        term = jnp.dot(hi, b_bf16, preferred_element_type=F32)
        acc = term if acc is None else acc + term
        rem = rem - hi.astype(F32)
    return acc


def _dot3(a, b_hi, b_lo):
    ah = a.astype(BF16)
    al = (a - ah.astype(F32)).astype(BF16)
    d = functools.partial(jnp.dot, preferred_element_type=F32)
    return d(ah, b_hi) + (d(ah, b_lo) + d(al, b_hi))


def _group_sum(x, group, mean=False):
    r = lax.broadcasted_iota(jnp.int32, (LANES, LANES), 0) // group
    c = lax.broadcasted_iota(jnp.int32, (LANES, LANES), 1) // group
    assert group & (group - 1) == 0
    weights = jnp.where(r == c, 1.0 / group if mean else 1.0, 0.0).astype(BF16)
    cols = [
        _split_dot(x[:, c0:c0 + LANES], weights, 2)
        for c0 in range(0, x.shape[1], LANES)
    ]
    return jnp.concatenate(cols, axis=1)


def _proj_body(head_ref, x_ref, g_ref, w_ref, o_ref, hn_ref):
    @pl.when(pl.program_id(1) == 0)
    def _():
        x = jnp.where(pl.program_id(0) == 0, head_ref[...], x_ref[...])
        ms = jnp.mean(x * x, axis=-1, keepdims=True)
        hn_ref[...] = (x * lax.rsqrt(ms + NORM_EPS) * g_ref[...]).astype(BF16)

    o_ref[...] = jnp.dot(hn_ref[...], w_ref[...], preferred_element_type=F32)


def _proj(head, x, gain, w, col_tile):
    d = x.shape[1]
    tp = head.shape[0] + x.shape[0]
    n = w.shape[1]
    return pl.pallas_call(
        _proj_body,
        grid=(tp // ROW_TILE, n // col_tile),
        in_specs=[
            pl.BlockSpec((ROW_TILE, d), lambda i, j: (0, 0)),
            pl.BlockSpec((ROW_TILE, d), lambda i, j: (jnp.maximum(i - 1, 0), 0)),
            pl.BlockSpec((1, d), lambda i, j: (0, 0)),
            pl.BlockSpec((d, col_tile), lambda i, j: (0, j)),
        ],
        out_specs=pl.BlockSpec((ROW_TILE, col_tile), lambda i, j: (i, j)),
        out_shape=jax.ShapeDtypeStruct((tp, n), F32),
        scratch_shapes=[pltpu.VMEM((ROW_TILE, d), BF16)],
        compiler_params=_cparams(("arbitrary", "arbitrary")),
        name="proj",
    )(head, x, gain, w)


def _softplus(z):
    return jnp.maximum(z, 0.0) + jnp.log(1.0 + jnp.exp(-jnp.abs(z)))


def _sigmoid(z):
    return 0.5 * jnp.tanh(0.5 * z) + 0.5


def _rwkv_pre_body(r_ref, k_ref, v_ref, l_ref, mu_ref, mul_ref, w0_ref, w2h_ref, w2l_ref, a0_ref, a2h_ref,
                   a2l_ref, g2h_ref, g2l_ref, kk_ref, ka_ref, rk_ref,
                   at_ref, rt_ref, bt_ref, kt_ref, vo_ref, bht_ref, kht_ref, gct_ref, g_ref, bonus_ref,
                   last_ref, lastl_ref):
    i = pl.program_id(0)

    @pl.when(i == 0)
    def _():
        last_ref[...] = jnp.zeros_like(last_ref)
        lastl_ref[...] = jnp.zeros_like(lastl_ref)

    rows = r_ref.shape[0]
    row = lax.broadcasted_iota(jnp.int32, (rows, 1), 0)

    def shift_mix(p, last_row, mu):
        prev = jnp.where(row == 0, last_row, pltpu.roll(p, 1, axis=0))
        return p + (prev - p) * mu

    w = RWKV_WIDTH
    r_raw, k_raw, v_raw, l_raw = r_ref[...], k_ref[...], v_ref[...], l_ref[...]
    r = shift_mix(r_raw, last_ref[0:1, 0:w], mu_ref[:, 0:w])
    k = shift_mix(k_raw, last_ref[0:1, w:2 * w], mu_ref[:, w:2 * w])
    v = shift_mix(v_raw, last_ref[0:1, 2 * w:3 * w], mu_ref[:, 2 * w:3 * w])
    lo = shift_mix(l_raw, lastl_ref[0:1, :], mul_ref[...])
    last_ref[0:1, 0:w] = r_raw[rows - 1:rows]
    last_ref[0:1, w:2 * w] = k_raw[rows - 1:rows]
    last_ref[0:1, 2 * w:3 * w] = v_raw[rows - 1:rows]
    lastl_ref[0:1, :] = l_raw[rows - 1:rows]

    c0, c1 = LORA_PAD[0], LORA_PAD[0] + LORA_PAD[1]
    wd, ad, gd = lo[:, 0:c0], lo[:, c0:c1], lo[:, c1:]
    wlog = -_softplus(-(w0_ref[...] + _dot3(jnp.tanh(wd), w2h_ref[...], w2l_ref[...]))) - 0.5
    logd = -jnp.exp(wlog)
    a = _sigmoid(a0_ref[...] + _dot3(ad, a2h_ref[...], a2l_ref[...]))
    g_ref[...] = _dot3(_sigmoid(gd), g2h_ref[...], g2l_ref[...])

    kk = k * kk_ref[...]
    kk = kk * jnp.minimum(lax.rsqrt(_group_sum(kk * kk, RWKV_HEAD_DIM)), 1e12)
    k = k * (1.0 + (a - 1.0) * ka_ref[...])
    b = kk * a
    bonus_ref[...] = _group_sum(r * k * rk_ref[...], RWKV_HEAD_DIM) * v

    tr = lax.broadcasted_iota(jnp.int32, (rows, rows), 0)
    tc = lax.broadcasted_iota(jnp.int32, (rows, rows), 1)
    same = (tr // CHUNK) == (tc // CHUNK)
    cum, tot = _split_dot_lhs([(same & (tc <= tr)).astype(BF16), same.astype(BF16)], logd)

    e_prev = jnp.exp(cum - logd)
    e_cum = jnp.exp(cum)
    e_neg = jnp.exp(-cum)
    e_rem = jnp.exp(tot - cum)
    at_ref[...] = (-kk * e_prev).astype(BF16)
    rt_ref[...] = (r * e_cum).astype(BF16)
    bt_ref[...] = (b * e_neg).astype(BF16)
    kt_ref[...] = (k * e_neg).astype(BF16)
    vo_ref[...] = v.astype(BF16)
    bht_ref[...] = (b * e_rem).T.astype(BF16)
    kht_ref[...] = (k * e_rem).T.astype(BF16)
    gct_ref[...] = jnp.exp(tot).T


def _split_dot_lhs(lhs_list, b):
    accs = [None] * len(lhs_list)
    rem = b
    for _ in range(3):
        hi = rem.astype(BF16)
        for n, a in enumerate(lhs_list):
            term = jnp.dot(a, hi, preferred_element_type=F32)
            accs[n] = term if accs[n] is None else accs[n] + term
        rem = rem - hi.astype(F32)
    return accs


def _rwkv_pre(p, mu, mul, w0, w2, a0, a2, g2, k_k, k_a, r_k):
    tp = p.shape[0]
    w = RWKV_WIDTH
    rows = RWKV_ROWS
    first = 3 * DIFF_WIDTH // w
    lora_blk = (3 * DIFF_WIDTH + 3 * w) // LORA_COLS

    def full(arr):
        return pl.BlockSpec(arr.shape, lambda i: (0, 0))

    tok = pl.BlockSpec((rows, w), lambda i: (i, 0))
    chan = pl.BlockSpec((w, rows), lambda i: (0, i))
    tok_bf = jax.ShapeDtypeStruct((tp, w), BF16)
    params = (mu, mul, w0, *w2, a0, *a2, *g2, k_k, k_a, r_k)
    return pl.pallas_call(
        _rwkv_pre_body,
        grid=(tp // rows,),
        in_specs=[
            pl.BlockSpec((rows, w), lambda i: (i, first)),
            pl.BlockSpec((rows, w), lambda i: (i, first + 1)),
            pl.BlockSpec((rows, w), lambda i: (i, first + 2)),
            pl.BlockSpec((rows, LORA_COLS), lambda i: (i, lora_blk)),
            *[full(a) for a in params],
        ],
        out_specs=[tok, tok, tok, tok, tok, chan, chan, chan, tok, tok],
        out_shape=[tok_bf, tok_bf, tok_bf, tok_bf, tok_bf,
                   jax.ShapeDtypeStruct((w, tp), BF16), jax.ShapeDtypeStruct((w, tp), BF16),
                   jax.ShapeDtypeStruct((w, tp), F32),
                   jax.ShapeDtypeStruct((tp, w), F32), jax.ShapeDtypeStruct((tp, w), F32)],
        scratch_shapes=[pltpu.VMEM((8, 3 * w), F32), pltpu.VMEM((8, LORA_COLS), F32)],
        compiler_params=_cparams(("arbitrary",)),
        name="rwkv_pre",
    )(p, p, p, p, *params)


def _rwkv_chunk_body(at_ref, rt_ref, bt_ref, kt_ref, v_ref, bht_ref, kht_ref, gct_ref, g_ref, bonus_ref,
                     lnw_ref, lnb_ref, y_ref, state_ref, yn_ref):
    n = RWKV_HEAD_DIM
    c = CHUNK

    @pl.when(pl.program_id(0) == 0)
    def _():
        state_ref[...] = jnp.zeros_like(state_ref)

    tr = lax.broadcasted_iota(jnp.int32, (2 * c, 2 * c), 0)
    tc = lax.broadcasted_iota(jnp.int32, (2 * c, 2 * c), 1) % c
    mask = ((tr < c) & (tc < tr)) | ((tr >= c) & (tc <= tr - c))
    er = lax.broadcasted_iota(jnp.int32, (c, 2 * c), 0)
    ec = lax.broadcasted_iota(jnp.int32, (c, 2 * c), 1)
    right = ec >= c
    eye2 = (ec == er + c).astype(F32)
    zeros_cn = jnp.zeros((c, n), BF16)
    dot = functools.partial(jnp.dot, preferred_element_type=F32)
    bf = lambda t: t.astype(BF16)

    heads = range(RWKV_HEADS)
    chunks = range(at_ref.shape[0] // c)
    units = [(ci, h) for ci in chunks for h in heads]

    def blk(ref, u):
        return ref[u[0] * c:(u[0] + 1) * c, u[1] * n:(u[1] + 1) * n]

    lhs = {u: jnp.concatenate([blk(at_ref, u), blk(rt_ref, u)], axis=0) for u in units}
    w = {}
    for u in units:
        rhs = jnp.concatenate([blk(bt_ref, u), blk(kt_ref, u)], axis=0)
        wu = lax.dot_general(lhs[u], rhs, (((1,), (1,)), ((), ())), preferred_element_type=F32)
        w[u] = jnp.where(mask, wu, 0.0)
    top = {u: bf(w[u][:c]) for u in units}
    bot = {u: bf(w[u][c:]) for u in units}
    xs = {u: jnp.where(right, eye2, w[u][:c]) for u in units}
    for _ in range(int(math.log2(c))):
        xb = {u: bf(xs[u]) for u in units}
        xs = {u: dot(xb[u][:, :c], xb[u]) + jnp.where(right, xs[u], 0.0) for u in units}
    invb = {u: bf(xs[u]) for u in units}

    for ci in chunks:
        us = [(ci, h) for h in heads]
        h0 = [state_ref[h] for h in heads]
        x0 = [dot(lhs[u], bf(h0[u[1]])) for u in us]
        v = [blk(v_ref, u) for u in us]
        sys_rhs = [x0[h][:c] + dot(top[(ci, h)], jnp.concatenate([zeros_cn, v[h]], axis=0)) for h in heads]
        u_sol = [dot(invb[(ci, h)], jnp.concatenate([zeros_cn, bf(sys_rhs[h])], axis=0)) for h in heads]
        uv = [jnp.concatenate([bf(u_sol[h]), v[h]], axis=0) for h in heads]
        for h in heads:
            rs = slice(ci * c, (ci + 1) * c)
            cs = slice(h * n, (h + 1) * n)
            yn_ref[rs, cs] = x0[h][c:] + dot(bot[(ci, h)], uv[h])
            bkt = jnp.concatenate([bht_ref[cs, rs], kht_ref[cs, rs]], axis=1)
            state_ref[h] = gct_ref[cs, rs] * h0[h] + dot(bkt, uv[h])

    y = yn_ref[...]
    d = y - _group_sum(y, n, mean=True)
    var = _group_sum(d * d, n, mean=True)
    yn = d * lax.rsqrt(var + RWKV_LN_EPS)
    y_ref[...] = ((yn * lnw_ref[...] + lnb_ref[...] + bonus_ref[...]) * g_ref[...]).astype(BF16)


def _rwkv_chunks(at, rt, bt, kt, v, bht, kht, gct, g, bonus, lnw, lnb):
    tp, w = at.shape
    rows = RWKV_ROWS
    tok = pl.BlockSpec((rows, w), lambda i: (i, 0))
    chan = pl.BlockSpec((w, rows), lambda i: (0, i))
    vec = pl.BlockSpec((1, w), lambda i: (0, 0))
    return pl.pallas_call(
        _rwkv_chunk_body,
        grid=(tp // rows,),
        in_specs=[tok, tok, tok, tok, tok, chan, chan, chan, tok, tok, vec, vec],
        out_specs=tok,
        out_shape=jax.ShapeDtypeStruct((tp, w), BF16),
        scratch_shapes=[pltpu.VMEM((RWKV_HEADS, RWKV_HEAD_DIM, RWKV_HEAD_DIM), F32),
                        pltpu.VMEM((rows, w), F32)],
        compiler_params=_cparams(("arbitrary",)),
        name="rwkv_chunks",
    )(at, rt, bt, kt, v, bht, kht, gct, g, bonus, lnw, lnb)


def _attn_pre_body(q_ref, k_ref, v_ref, qg_ref, kg_ref, qz_ref, kn_ref, vt_ref):
    d = DIFF_QK_DIM

    def qk_norm(x, gain):
        return x * lax.rsqrt(_group_sum(x * x, d, mean=True) + NORM_EPS) * gain

    q = qk_norm(q_ref[...], qg_ref[...])
    first = (lax.broadcasted_iota(jnp.int32, q.shape, 1) % (2 * d)) < d
    qz_ref[0] = jnp.where(first, q, 0.0).astype(BF16)
    qz_ref[1] = jnp.where(first, 0.0, q).astype(BF16)
    kn_ref[...] = qk_norm(k_ref[...], kg_ref[...]).astype(BF16)
    vt = v_ref[...].T.astype(BF16)
    e = DIFF_V_DIM
    fill_row = lax.broadcasted_iota(jnp.int32, (VT_ROWS - e, vt.shape[1]), 0)
    fill = jnp.where(fill_row == 0, 1.0, 0.0).astype(BF16)
    for h in range(DIFF_HEADS):
        vt_ref[h * VT_ROWS:h * VT_ROWS + e, :] = vt[h * e:(h + 1) * e]
        vt_ref[h * VT_ROWS + e:(h + 1) * VT_ROWS, :] = fill


def _attn_pre(p, q_gain, k_gain):
    tp = p.shape[0]
    w = DIFF_WIDTH
    vec = pl.BlockSpec((1, w), lambda i: (0, 0))
    blk = lambda j: pl.BlockSpec((ROW_TILE, w), lambda i: (i, j))
    return pl.pallas_call(
        _attn_pre_body,
        grid=(tp // ROW_TILE,),
        in_specs=[blk(0), blk(1), blk(2), vec, vec],
        out_specs=[pl.BlockSpec((2, ROW_TILE, w), lambda i: (0, i, 0)), blk(0),
                   pl.BlockSpec((DIFF_HEADS * VT_ROWS, ROW_TILE), lambda i: (0, i))],
        out_shape=[jax.ShapeDtypeStruct((2, tp, w), BF16), jax.ShapeDtypeStruct((tp, w), BF16),
                   jax.ShapeDtypeStruct((DIFF_HEADS * VT_ROWS, tp), BF16)],
        compiler_params=_cparams(("arbitrary",)),
        name="attn_pre",
    )(p, p, p, q_gain, k_gain)


FLASH_HEADS = 8
VT_ROWS = DIFF_V_DIM + 16
FLASH_QCOLS = 256


def _flash_body(qi_ref, ki_ref, path_ref, qz_ref, k_ref, vt_ref, lam_ref, sw_ref, o_ref, m_ref, acc_ref,
                *, first_key, lambda_init):
    step = pl.program_id(1)
    qi = qi_ref[step]
    ki = ki_ref[step]
    path = path_ref[step]
    tq = o_ref.shape[0]
    tk = k_ref.shape[0]
    e = DIFF_V_DIM
    heads = range(FLASH_HEADS)
    lanes = [slice(g * e, (g + 1) * e) for g in heads]

    @pl.when(ki == 0)
    def _():
        m_ref[...] = jnp.full_like(m_ref, -jnp.inf)
        acc_ref[...] = jnp.zeros_like(acc_ref)

    def strand(g, c0, rows, mask_rows):
        cols = slice(c0, c0 + FLASH_QCOLS)
        which, off = divmod(c0, tq)
        st = {}

        def stage_scores():
            q = qz_ref[which, off:off + FLASH_QCOLS, lanes[g]]
            s = lax.dot_general(k_ref[rows, lanes[g]], q, (((1,), (1,)), ((), ())),
                                preferred_element_type=F32)
            if mask_rows is not None:
                n = mask_rows.stop - mask_rows.start
                k_pos = ki * tk + mask_rows.start + lax.broadcasted_iota(jnp.int32, (n, FLASH_QCOLS), 0)
                q_pos = qi * tq + off + lax.broadcasted_iota(jnp.int32, (n, FLASH_QCOLS), 1)
                vis = (k_pos <= q_pos) & ((k_pos >= first_key) | (q_pos < first_key))
                lo, hi = mask_rows.start - rows.start, mask_rows.stop - rows.start
                parts = [s[:lo], jnp.where(vis, s[lo:hi], -jnp.inf), s[hi:]]
                s = jnp.concatenate([part for part in parts if part.shape[0]], axis=0)
            st["s"] = s

        def stage_softmax():
            m_prev = m_ref[g, :, cols]
            m_new = jnp.maximum(m_prev, jnp.max(st["s"], axis=0, keepdims=True))
            st["alpha"] = jnp.exp2(m_prev - m_new)
            p = jnp.exp2(st["s"] - m_new)
            m_ref[g, :, cols] = m_new
            st["p"] = p.astype(BF16)

        def stage_values():
            acc_ref[g, :, cols] = st["alpha"] * acc_ref[g, :, cols] + jnp.dot(
                vt_ref[g * VT_ROWS:(g + 1) * VT_ROWS, rows], st["p"], preferred_element_type=F32)

        return [stage_scores, stage_softmax, stage_values]

    def run(rows_for):
        strands = [strand(g, c0, *rows_for(c0 % tq))
                   for c0 in range(0, 2 * tq, FLASH_QCOLS) for g in heads]
        depth = len(strands[0])
        for t in range(len(strands) + depth - 1):
            for stage in range(depth):
                if 0 <= t - stage < len(strands):
                    strands[t - stage][stage]()

    tail = first_key // LANES * LANES
    paths = {
        _PATH_DIAG_FIRST_HALF: lambda off: (slice(0, off + FLASH_QCOLS), slice(0, off + FLASH_QCOLS)),
        _PATH_DIAG_SECOND_HALF: lambda off: (slice(0, tq + off + FLASH_QCOLS), slice(tq, tq + off + FLASH_QCOLS)),
        _PATH_PAD_AND_DIAG: lambda off: (slice(tail, tq + off + FLASH_QCOLS), slice(tail, tq + off + FLASH_QCOLS)),
        _PATH_PAD: lambda off: (slice(tail, tk), slice(tail, tq)),
        _PATH_FULL: lambda off: (slice(0, tk), None),
    }
    for pid, rows_for in paths.items():
        pl.when(path == pid)(functools.partial(run, rows_for))

    @pl.when(path <= _PATH_PAD_AND_DIAG)
    def _():
        lp = lam_ref[...]
        lam = (jnp.exp(jnp.sum(lp[0:1] * lp[1:2])) - jnp.exp(jnp.sum(lp[2:3] * lp[3:4])) + lambda_init)
        for g in heads:
            o = acc_ref[g, :e] / acc_ref[g, e:e + 1]
            o = o[:, :tq] - lam * o[:, tq:]
            ms = jnp.mean(o * o, axis=0, keepdims=True)
            o = (o * lax.rsqrt(ms + SUBLN_EPS)).T * (sw_ref[...] * (1.0 - lambda_init))
            o_ref[:, lanes[g]] = o.astype(BF16)


_PATH_DIAG_FIRST_HALF = 0
_PATH_DIAG_SECOND_HALF = 1
_PATH_PAD_AND_DIAG = 2
_PATH_PAD = 3
_PATH_FULL = 4


def _flash(qz, kn, vt, lam_p, subln_w, first_key, lambda_init):
    _, tp, w = qz.shape
    e = DIFF_V_DIM
    we = FLASH_HEADS * e
    tk = 2 * ROW_TILE
    assert first_key < ROW_TILE
    steps = []
    for qi in range(tp // ROW_TILE):
        last = qi // 2
        for ki in range(last + 1):
            if ki == last:
                path = (_PATH_DIAG_FIRST_HALF if qi % 2 == 0 else
                        _PATH_PAD_AND_DIAG if ki == 0 else _PATH_DIAG_SECOND_HALF)
            else:
                path = _PATH_PAD if ki == 0 else _PATH_FULL
            steps.append((qi, ki, path))
    tabs = [jnp.asarray(np.array([s[n] for s in steps], np.int32)) for n in range(3)]
    grid_spec = pltpu.PrefetchScalarGridSpec(
        num_scalar_prefetch=3,
        grid=(DIFF_HEADS // FLASH_HEADS, len(steps)),
        in_specs=[
            pl.BlockSpec((2, ROW_TILE, we), lambda h, s, qt, kt, pt: (0, qt[s], h)),
            pl.BlockSpec((tk, we), lambda h, s, qt, kt, pt: (kt[s], h)),
            pl.BlockSpec((FLASH_HEADS * VT_ROWS, tk), lambda h, s, qt, kt, pt: (h, kt[s])),
            pl.BlockSpec(lam_p.shape, lambda h, s, qt, kt, pt: (0, 0)),
            pl.BlockSpec((1, e), lambda h, s, qt, kt, pt: (0, 0)),
        ],
        out_specs=pl.BlockSpec((ROW_TILE, we), lambda h, s, qt, kt, pt: (qt[s], h)),
        scratch_shapes=[pltpu.VMEM((FLASH_HEADS, 1, 2 * ROW_TILE), F32),
                        pltpu.VMEM((FLASH_HEADS, VT_ROWS, 2 * ROW_TILE), F32)],
    )
    return pl.pallas_call(
        functools.partial(_flash_body, first_key=first_key, lambda_init=lambda_init),
        grid_spec=grid_spec,
        out_shape=jax.ShapeDtypeStruct((tp, w), BF16),
        compiler_params=_cparams(("arbitrary", "arbitrary")),
        name="diff_flash",
    )(*tabs, qz, kn, vt, lam_p, subln_w)


def _out_proj_body(ya_ref, yb_ref, wa_ref, wb_ref, head_ref, x_ref, g_ref, h2_ref, hn_ref, *, head_blocks):
    acc = jnp.dot(ya_ref[...], wa_ref[...], preferred_element_type=F32)
    acc = acc + jnp.dot(yb_ref[...], wb_ref[...], preferred_element_type=F32)
    h2 = jnp.where(pl.program_id(0) < head_blocks, head_ref[...], x_ref[...]) + acc
    h2_ref[...] = h2
    ms = jnp.mean(h2 * h2, axis=-1, keepdims=True)
    hn_ref[...] = (h2 * lax.rsqrt(ms + NORM_EPS) * g_ref[...]).astype(BF16)


def _out_proj(ya, yb, wa, wb, head, x, gain, row_tile):
    d = x.shape[1]
    tp = head.shape[0] + x.shape[0]
    nhb = head.shape[0] // row_tile
    rows = pl.BlockSpec((row_tile, d), lambda i: (i, 0))
    return pl.pallas_call(
        functools.partial(_out_proj_body, head_blocks=nhb),
        grid=(tp // row_tile,),
        in_specs=[
            pl.BlockSpec((row_tile, ya.shape[1]), lambda i: (i, 0)),
            pl.BlockSpec((row_tile, yb.shape[1]), lambda i: (i, 0)),
            pl.BlockSpec(wa.shape, lambda i: (0, 0)),
            pl.BlockSpec(wb.shape, lambda i: (0, 0)),
            pl.BlockSpec((row_tile, d), lambda i: (jnp.minimum(i, nhb - 1), 0)),
            pl.BlockSpec((row_tile, d), lambda i: (jnp.maximum(i - nhb, 0), 0)),
            pl.BlockSpec((1, d), lambda i: (0, 0)),
        ],
        out_specs=[rows, rows],
        out_shape=[jax.ShapeDtypeStruct((tp, d), F32), jax.ShapeDtypeStruct((tp, d), BF16)],
        compiler_params=_cparams(("arbitrary",)),
        name="out_proj",
    )(ya, yb, wa, wb, head, x, gain)


FFN_SUB = 256


def _ffn_body(x_ref, wg_ref, wu_ref, cwg_ref, cwu_ref, cbg_ref, cbu_ref, wd_ref, h2_ref, o_ref,
              acc_ref, ua_ref, ub_ref, *, nf):
    s = pl.program_id(0)
    halo = x_ref.shape[0] - o_ref.shape[0]
    tile = jnp.maximum(s - 1, 0) % nf
    subs = [slice(c, c + FFN_SUB) for c in range(0, wg_ref.shape[1], FFN_SUB)]

    @pl.when(s == 0)
    def _():
        ua_ref[...] = jnp.zeros_like(ua_ref)
        ub_ref[...] = jnp.zeros_like(ub_ref)

    def conv(u, cw_ref, cb_ref, cols):
        cw = cw_ref[:, cols]
        u1 = pltpu.roll(u, 1, axis=0)
        y = pltpu.roll(u1, 1, axis=0) * cw[0:1] + u1 * cw[1:2] + u * cw[2:3]
        return y[halo:] + cb_ref[:, cols]

    def step(u_next, u_prev):
        x = x_ref[...]
        for c in subs:
            u_next[0, :, c] = jnp.dot(x, wg_ref[:, c], preferred_element_type=F32)
            u_next[1, :, c] = jnp.dot(x, wu_ref[:, c], preferred_element_type=F32)
        part = None
        for c in subs:
            gate = conv(u_prev[0, :, c], cwg_ref, cbg_ref, c)
            up = conv(u_prev[1, :, c], cwu_ref, cbu_ref, c)
            act = (gate * _sigmoid(gate) * up).astype(BF16)
            term = jnp.dot(act, wd_ref[c, :], preferred_element_type=F32)
            part = term if part is None else part + term
        acc_ref[...] = jnp.where(tile == 0, part, acc_ref[...] + part)

    @pl.when(s % 2 == 0)
    def _():
        step(ua_ref, ub_ref)

    @pl.when(s % 2 == 1)
    def _():
        step(ub_ref, ua_ref)

    @pl.when((s > 0) & (tile == nf - 1))
    def _():
        o_ref[...] = h2_ref[...] + acc_ref[...]


def _ffn(hn, h2, w_up, conv_w, conv_b, w_down, first_block, n_out_rows, ff_tile):
    tp, d = hn.shape
    dff = w_down.shape[0]
    nf = dff // ff_tile
    halo = 16
    fb = first_block
    items = (n_out_rows // ROW_TILE) * nf

    def up_item(s):
        it = jnp.minimum(s, items - 1)
        return it // nf, it % nf

    def down_item(s):
        it = jnp.maximum(s - 1, 0)
        return it // nf, it % nf

    return pl.pallas_call(
        functools.partial(_ffn_body, nf=nf),
        grid=(items + 1,),
        in_specs=[
            pl.BlockSpec((pl.Element(ROW_TILE + halo), pl.Element(d)),
                         lambda s: (pl.multiple_of((up_item(s)[0] + fb) * ROW_TILE - halo, halo), 0)),
            pl.BlockSpec((d, ff_tile), lambda s: (0, up_item(s)[1])),
            pl.BlockSpec((d, ff_tile), lambda s: (0, up_item(s)[1] + nf)),
            pl.BlockSpec((CONV_WIDTH, ff_tile), lambda s: (0, down_item(s)[1])),
            pl.BlockSpec((CONV_WIDTH, ff_tile), lambda s: (0, down_item(s)[1] + nf)),
            pl.BlockSpec((1, ff_tile), lambda s: (0, down_item(s)[1])),
            pl.BlockSpec((1, ff_tile), lambda s: (0, down_item(s)[1] + nf)),
            pl.BlockSpec((ff_tile, d), lambda s: (down_item(s)[1], 0)),
            pl.BlockSpec((ROW_TILE, d), lambda s: (down_item(s)[0] + fb, 0)),
        ],
        out_specs=pl.BlockSpec((ROW_TILE, d), lambda s: (down_item(s)[0], 0)),
        out_shape=jax.ShapeDtypeStruct((n_out_rows, d), F32),
        scratch_shapes=[pltpu.VMEM((ROW_TILE, d), F32),
                        pltpu.VMEM((2, ROW_TILE + halo, ff_tile), F32),
                        pltpu.VMEM((2, ROW_TILE + halo, ff_tile), F32)],
        compiler_params=_cparams(("arbitrary",)),
        name="conv_glu_ffn",
    )(hn, w_up, w_up, conv_w, conv_w, conv_b, conv_b, w_down, h2)


def _pad_cols(a, width):
    return jnp.pad(a, ((0, 0), (0, width - a.shape[1])))


def _layer(head, x, first_key, lambda_init, norm1_g, w_in, mu, w0, w2, a0, a2, g2, k_k, k_a, r_k, lnx_w, lnx_b,
           q_gain, k_gain, lam_q1, lam_k1, lam_q2, lam_k2, subln_w, w_out, norm2_g, w_up, conv_w, conv_b,
           w_down):
    w = RWKV_WIDTH
    row = lambda a: a.reshape(1, -1)
    n_rkv = 3 * w
    cuts = np.cumsum((n_rkv,) + LORA_SIZES)
    segs = [w_in[:, cuts[-1]:], w_in[:, :n_rkv]]
    segs += [_pad_cols(w_in[:, lo:hi], wd) for lo, hi, wd in zip(cuts[:-1], cuts[1:], LORA_PAD)]
    segs.append(jnp.zeros((w_in.shape[0], PROJ_COLS - sum(s.shape[1] for s in segs)), w_in.dtype))
    w_proj = jnp.concatenate(segs, axis=1).astype(BF16)
    mu_row = row(mu)
    mu_l = jnp.concatenate([_pad_cols(mu_row[:, lo:hi], wd) for lo, hi, wd in zip(cuts[:-1], cuts[1:], LORA_PAD)],
                           axis=1)

    def lora_weight(wt, rows):
        full = jnp.pad(wt, ((0, rows - wt.shape[0]), (0, 0)))
        hi = full.astype(BF16)
        return hi, (full - hi.astype(F32)).astype(BF16)

    p = _proj(head, x, row(norm1_g), w_proj, PROJ_COLS // 3)

    at, rt, bt, kt, v, bht, kht, gct, g, bonus = _rwkv_pre(
        p, mu_row[:, :n_rkv], mu_l, row(w0), lora_weight(w2, LORA_PAD[0]), row(a0),
        lora_weight(a2, LORA_PAD[1]), lora_weight(g2, LORA_PAD[2]), row(k_k), row(k_a), row(r_k))
    y_a = _rwkv_chunks(at, rt, bt, kt, v, bht, kht, gct, g, bonus, row(lnx_w), row(lnx_b))

    tile_heads = lambda a: jnp.tile(row(a), (1, 2 * DIFF_HEADS))
    q_scale = DIFF_QK_DIM ** -0.5 * math.log2(math.e)
    qz, kn, vt = _attn_pre(p, tile_heads(q_gain) * q_scale, tile_heads(k_gain))
    lam_p = jnp.stack([lam_q1, lam_k1, lam_q2, lam_k2]).astype(F32)
    y_b = _flash(qz, kn, vt, lam_p, row(subln_w), first_key, lambda_init)

    w_out_b = w_out.astype(BF16)
    h2, hn2 = _out_proj(y_a, y_b, w_out_b[:w], w_out_b[w:], head, x, row(norm2_g), ROW_TILE)

    return _ffn(hn2, h2, w_up.astype(BF16), conv_w, row(conv_b), w_down.astype(BF16), 1, x.shape[0], 512)


def kernel(x, meta, norm1_g, w_in, mu, w0, w2, a0, a2, g2, k_k, k_a, r_k, lnx_w, lnx_b, q_gain, k_gain,
           lam_q1, lam_k1, lam_q2, lam_k2, subln_w, w_out, norm2_g, w_up, conv_w, conv_b, w_down):
    batch, seq, d = x.shape
    depth = norm1_g.shape[0]
    assert depth == 1 and seq % ROW_TILE == 0 and meta.shape[0] == N_META
    front = ROW_TILE - N_META
    head = jnp.concatenate([jnp.zeros((front, d), x.dtype), meta.astype(x.dtype)], axis=0)
    lambda_init = 0.8 - 0.6 * math.exp(-0.3 * 0)
    outs = [
        _layer(head, x[bi], front, lambda_init, norm1_g[0], w_in[0], mu[0], w0[0], w2[0], a0[0], a2[0], g2[0],
               k_k[0], k_a[0], r_k[0], lnx_w[0], lnx_b[0], q_gain[0], k_gain[0], lam_q1[0], lam_k1[0],
               lam_q2[0], lam_k2[0], subln_w[0], w_out[0], norm2_g[0], w_up[0], conv_w[0], conv_b[0], w_down[0])
        for bi in range(batch)]
    return outs[0][None] if batch == 1 else jnp.stack(outs, axis=0)
```

```python
import functools
import math

import jax
import jax.numpy as jnp
import numpy as np
from jax import lax
from jax.experimental import pallas as pl
from jax.experimental.pallas import tpu as pltpu

F32 = jnp.float32
BF16 = jnp.bfloat16

N_META = 16
RWKV_HEADS = 16
RWKV_HEAD_DIM = 64
RWKV_WIDTH = RWKV_HEADS * RWKV_HEAD_DIM
DECAY_LORA = 64
AAA_LORA = 64
GATE_LORA = 160
RWKV_LN_EPS = 64e-5
DIFF_HEADS = 8
DIFF_QK_DIM = 64
DIFF_V_DIM = 128
DIFF_WIDTH = DIFF_HEADS * DIFF_V_DIM
CONV_WIDTH = 3
NORM_EPS = 1e-6
SUBLN_EPS = 1e-5

LANES = 128
ROW_TILE = 512
CHUNK = 64
RWKV_ROWS = 4 * CHUNK
RWKV_CHUNK_ROWS = 8 * CHUNK
LORA_SIZES = (DECAY_LORA, AAA_LORA, GATE_LORA)
LORA_PAD = (128, 128, 256)
LORA_COLS = sum(LORA_PAD)
MXU_COLS = 256
PROJ_COLS = -(-(3 * DIFF_WIDTH + 3 * RWKV_WIDTH + LORA_COLS) // (3 * MXU_COLS)) * (3 * MXU_COLS)
VMEM_LIMIT = 56 * 1024 * 1024


def _cparams(sem):
    return pltpu.CompilerParams(dimension_semantics=sem, vmem_limit_bytes=VMEM_LIMIT)


def _split_dot(a, b_bf16, parts):
    acc = None
    rem = a
    for _ in range(parts):
        hi = rem.astype(BF16)
        term = jnp.dot(hi, b_bf16, preferred_element_type=F32)
        acc = term if acc is None else acc + term
        rem = rem - hi.astype(F32)
    return acc


def _dot3(a, b_hi, b_lo):
    ah = a.astype(BF16)
    al = (a - ah.astype(F32)).astype(BF16)
    d = functools.partial(jnp.dot, preferred_element_type=F32)
    return d(ah, b_hi) + (d(ah, b_lo) + d(al, b_hi))


def _group_ones(n, group):
    r = lax.broadcasted_iota(jnp.int32, (n, n), 0) // group
    c = lax.broadcasted_iota(jnp.int32, (n, n), 1) // group
    return (r == c).astype(BF16)


def _group_sum(x, group):
    ones = _group_ones(LANES, group)
    cols = [
        _split_dot(x[:, c:c + LANES], ones, 2)
        for c in range(0, x.shape[1], LANES)
    ]
    return jnp.concatenate(cols, axis=1)


def _proj_body(head_ref, x_ref, g_ref, w_ref, o_ref, hn_ref):
    @pl.when(pl.program_id(1) == 0)
    def _():
        x = jnp.where(pl.program_id(0) == 0, head_ref[...], x_ref[...])
        ms = jnp.mean(x * x, axis=-1, keepdims=True)
        hn_ref[...] = (x * lax.rsqrt(ms + NORM_EPS) * g_ref[...]).astype(BF16)

    o_ref[...] = jnp.dot(hn_ref[...], w_ref[...], preferred_element_type=F32)


def _proj(head, x, gain, w, col_tile):
    d = x.shape[1]
    tp = head.shape[0] + x.shape[0]
    n = w.shape[1]
    return pl.pallas_call(
        _proj_body,
        grid=(tp // ROW_TILE, n // col_tile),
        in_specs=[
            pl.BlockSpec((ROW_TILE, d), lambda i, j: (0, 0)),
            pl.BlockSpec((ROW_TILE, d), lambda i, j: (jnp.maximum(i - 1, 0), 0)),
            pl.BlockSpec((1, d), lambda i, j: (0, 0)),
            pl.BlockSpec((d, col_tile), lambda i, j: (0, j)),
        ],
        out_specs=pl.BlockSpec((ROW_TILE, col_tile), lambda i, j: (i, j)),
        out_shape=jax.ShapeDtypeStruct((tp, n), F32),
        scratch_shapes=[pltpu.VMEM((ROW_TILE, d), BF16)],
        compiler_params=_cparams(("arbitrary", "arbitrary")),
        name="proj",
    )(head, x, gain, w)


def _softplus(z):
    return jnp.maximum(z, 0.0) + jnp.log(1.0 + jnp.exp(-jnp.abs(z)))


def _sigmoid(z):
    return 0.5 * jnp.tanh(0.5 * z) + 0.5


def _rwkv_pre_body(r_ref, k_ref, v_ref, l_ref, mu_ref, mul_ref, w0_ref, w2h_ref, w2l_ref, a0_ref, a2h_ref,
                   a2l_ref, g2h_ref, g2l_ref, kk_ref, ka_ref, rk_ref,
                   at_ref, rt_ref, bt_ref, kt_ref, vo_ref, bht_ref, kht_ref, gct_ref, g_ref, bonus_ref,
                   last_ref, lastl_ref):
    i = pl.program_id(0)

    @pl.when(i == 0)
    def _():
        last_ref[...] = jnp.zeros_like(last_ref)
        lastl_ref[...] = jnp.zeros_like(lastl_ref)

    rows = r_ref.shape[0]
    row = lax.broadcasted_iota(jnp.int32, (rows, 1), 0)

    def shift_mix(p, last_row, mu):
        prev = jnp.where(row == 0, last_row, pltpu.roll(p, 1, axis=0))
        return p + (prev - p) * mu

    w = RWKV_WIDTH
    r_raw, k_raw, v_raw, l_raw = r_ref[...], k_ref[...], v_ref[...], l_ref[...]
    r = shift_mix(r_raw, last_ref[0:1, 0:w], mu_ref[:, 0:w])
    k = shift_mix(k_raw, last_ref[0:1, w:2 * w], mu_ref[:, w:2 * w])
    v = shift_mix(v_raw, last_ref[0:1, 2 * w:3 * w], mu_ref[:, 2 * w:3 * w])
    lo = shift_mix(l_raw, lastl_ref[0:1, :], mul_ref[...])
    last_ref[0:1, 0:w] = r_raw[rows - 1:rows]
    last_ref[0:1, w:2 * w] = k_raw[rows - 1:rows]
    last_ref[0:1, 2 * w:3 * w] = v_raw[rows - 1:rows]
    lastl_ref[0:1, :] = l_raw[rows - 1:rows]

    c0, c1 = LORA_PAD[0], LORA_PAD[0] + LORA_PAD[1]
    wd, ad, gd = lo[:, 0:c0], lo[:, c0:c1], lo[:, c1:]
    wlog = -_softplus(-(w0_ref[...] + _dot3(jnp.tanh(wd), w2h_ref[...], w2l_ref[...]))) - 0.5
    logd = -jnp.exp(wlog)
    a = _sigmoid(a0_ref[...] + _dot3(ad, a2h_ref[...], a2l_ref[...]))
    g_ref[...] = _dot3(_sigmoid(gd), g2h_ref[...], g2l_ref[...])

    kk = k * kk_ref[...]
    kk = kk * jnp.minimum(lax.rsqrt(_group_sum(kk * kk, RWKV_HEAD_DIM)), 1e12)
    k = k * (1.0 + (a - 1.0) * ka_ref[...])
    b = kk * a
    bonus_ref[...] = _group_sum(r * k * rk_ref[...], RWKV_HEAD_DIM) * v

    tr = lax.broadcasted_iota(jnp.int32, (rows, rows), 0)
    tc = lax.broadcasted_iota(jnp.int32, (rows, rows), 1)
    same = (tr // CHUNK) == (tc // CHUNK)
    cum, tot = _split_dot_lhs([(same & (tc <= tr)).astype(BF16), same.astype(BF16)], logd)

    e_prev = jnp.exp(cum - logd)
    e_cum = jnp.exp(cum)
    e_neg = jnp.exp(-cum)
    e_rem = jnp.exp(tot - cum)
    at_ref[...] = (-kk * e_prev).astype(BF16)
    rt_ref[...] = (r * e_cum).astype(BF16)
    bt_ref[...] = (b * e_neg).astype(BF16)
    kt_ref[...] = (k * e_neg).astype(BF16)
    vo_ref[...] = v.astype(BF16)
    bht_ref[...] = (b * e_rem).T.astype(BF16)
    kht_ref[...] = (k * e_rem).T.astype(BF16)
    gct_ref[...] = jnp.exp(tot).T


def _split_dot_lhs(lhs_list, b):
    accs = [None] * len(lhs_list)
    rem = b
    for _ in range(3):
        hi = rem.astype(BF16)
        for n, a in enumerate(lhs_list):
            term = jnp.dot(a, hi, preferred_element_type=F32)
            accs[n] = term if accs[n] is None else accs[n] + term
        rem = rem - hi.astype(F32)
    return accs


def _rwkv_pre(p, mu, mul, w0, w2, a0, a2, g2, k_k, k_a, r_k):
    tp = p.shape[0]
    w = RWKV_WIDTH
    rows = RWKV_ROWS
    first = 3 * DIFF_WIDTH // w
    lora_blk = (3 * DIFF_WIDTH + 3 * w) // LORA_COLS

    def full(arr):
        return pl.BlockSpec(arr.shape, lambda i: (0, 0))

    tok = pl.BlockSpec((rows, w), lambda i: (i, 0))
    chan = pl.BlockSpec((w, rows), lambda i: (0, i))
    tok_bf = jax.ShapeDtypeStruct((tp, w), BF16)
    params = (mu, mul, w0, *w2, a0, *a2, *g2, k_k, k_a, r_k)
    return pl.pallas_call(
        _rwkv_pre_body,
        grid=(tp // rows,),
        in_specs=[
            pl.BlockSpec((rows, w), lambda i: (i, first)),
            pl.BlockSpec((rows, w), lambda i: (i, first + 1)),
            pl.BlockSpec((rows, w), lambda i: (i, first + 2)),
            pl.BlockSpec((rows, LORA_COLS), lambda i: (i, lora_blk)),
            *[full(a) for a in params],
        ],
        out_specs=[tok, tok, tok, tok, tok, chan, chan, chan, tok, tok],
        out_shape=[tok_bf, tok_bf, tok_bf, tok_bf, tok_bf,
                   jax.ShapeDtypeStruct((w, tp), BF16), jax.ShapeDtypeStruct((w, tp), BF16),
                   jax.ShapeDtypeStruct((w, tp), F32),
                   jax.ShapeDtypeStruct((tp, w), F32), jax.ShapeDtypeStruct((tp, w), F32)],
        scratch_shapes=[pltpu.VMEM((8, 3 * w), F32), pltpu.VMEM((8, LORA_COLS), F32)],
        compiler_params=_cparams(("arbitrary",)),
        name="rwkv_pre",
    )(p, p, p, p, *params)


def _rwkv_chunk_body(at_ref, rt_ref, bt_ref, kt_ref, v_ref, bht_ref, kht_ref, gct_ref, g_ref, bonus_ref,
                     lnw_ref, lnb_ref, y_ref, state_ref, yn_ref):
    n = RWKV_HEAD_DIM
    c = CHUNK

    @pl.when(pl.program_id(0) == 0)
    def _():
        state_ref[...] = jnp.zeros_like(state_ref)

    tr = lax.broadcasted_iota(jnp.int32, (2 * c, 2 * c), 0)
    tc = lax.broadcasted_iota(jnp.int32, (2 * c, 2 * c), 1) % c
    mask = ((tr < c) & (tc < tr)) | ((tr >= c) & (tc <= tr - c))
    er = lax.broadcasted_iota(jnp.int32, (c, 2 * c), 0)
    ec = lax.broadcasted_iota(jnp.int32, (c, 2 * c), 1)
    right = ec >= c
    eye2 = (ec == er + c).astype(F32)
    zeros_cn = jnp.zeros((c, n), BF16)
    dot = functools.partial(jnp.dot, preferred_element_type=F32)
    bf = lambda t: t.astype(BF16)

    heads = range(RWKV_HEADS)
    chunks = range(at_ref.shape[0] // c)
    units = [(ci, h) for ci in chunks for h in heads]

    def blk(ref, u):
        return ref[u[0] * c:(u[0] + 1) * c, u[1] * n:(u[1] + 1) * n]

    lhs = {u: jnp.concatenate([blk(at_ref, u), blk(rt_ref, u)], axis=0) for u in units}
    w = {}
    for u in units:
        rhs = jnp.concatenate([blk(bt_ref, u), blk(kt_ref, u)], axis=0)
        wu = lax.dot_general(lhs[u], rhs, (((1,), (1,)), ((), ())), preferred_element_type=F32)
        w[u] = jnp.where(mask, wu, 0.0)
    top = {u: bf(w[u][:c]) for u in units}
    bot = {u: bf(w[u][c:]) for u in units}
    xs = {u: jnp.where(right, eye2, w[u][:c]) for u in units}
    for _ in range(int(math.log2(c))):
        xb = {u: bf(xs[u]) for u in units}
        xs = {u: dot(xb[u][:, :c], xb[u]) + jnp.where(right, xs[u], 0.0) for u in units}
    invb = {u: bf(xs[u]) for u in units}

    for ci in chunks:
        us = [(ci, h) for h in heads]
        h0 = [state_ref[h] for h in heads]
        x0 = [dot(lhs[u], bf(h0[u[1]])) for u in us]
        v = [blk(v_ref, u) for u in us]
        sys_rhs = [x0[h][:c] + dot(top[(ci, h)], jnp.concatenate([zeros_cn, v[h]], axis=0)) for h in heads]
        u_sol = [dot(invb[(ci, h)], jnp.concatenate([zeros_cn, bf(sys_rhs[h])], axis=0)) for h in heads]
        uv = [jnp.concatenate([bf(u_sol[h]), v[h]], axis=0) for h in heads]
        for h in heads:
            rs = slice(ci * c, (ci + 1) * c)
            cs = slice(h * n, (h + 1) * n)
            yn_ref[rs, cs] = x0[h][c:] + dot(bot[(ci, h)], uv[h])
            bkt = jnp.concatenate([bht_ref[cs, rs], kht_ref[cs, rs]], axis=1)
            state_ref[h] = gct_ref[cs, rs] * h0[h] + dot(bkt, uv[h])

    y = yn_ref[...]
    inv_n = 1.0 / n
    d = y - _group_sum(y, n) * inv_n
    var = _group_sum(d * d, n) * inv_n
    yn = d * lax.rsqrt(var + RWKV_LN_EPS)
    y_ref[...] = ((yn * lnw_ref[...] + lnb_ref[...] + bonus_ref[...]) * g_ref[...]).astype(BF16)


def _rwkv_chunks(at, rt, bt, kt, v, bht, kht, gct, g, bonus, lnw, lnb):
    tp, w = at.shape
    rows = RWKV_CHUNK_ROWS
    tok = pl.BlockSpec((rows, w), lambda i: (i, 0))
    chan = pl.BlockSpec((w, rows), lambda i: (0, i))
    vec = pl.BlockSpec((1, w), lambda i: (0, 0))
    return pl.pallas_call(
        _rwkv_chunk_body,
        grid=(tp // rows,),
        in_specs=[tok, tok, tok, tok, tok, chan, chan, chan, tok, tok, vec, vec],
        out_specs=tok,
        out_shape=jax.ShapeDtypeStruct((tp, w), BF16),
        scratch_shapes=[pltpu.VMEM((RWKV_HEADS, RWKV_HEAD_DIM, RWKV_HEAD_DIM), F32),
                        pltpu.VMEM((rows, w), F32)],
        compiler_params=_cparams(("arbitrary",)),
        name="rwkv_chunks",
    )(at, rt, bt, kt, v, bht, kht, gct, g, bonus, lnw, lnb)


def _attn_pre_body(q_ref, k_ref, v_ref, qg_ref, kg_ref, qz_ref, kn_ref, vt_ref):
    d = DIFF_QK_DIM

    def qk_norm(x, gain):
        ms = _group_sum(x * x, d) * (1.0 / d)
        return x * lax.rsqrt(ms + NORM_EPS) * gain

    q = qk_norm(q_ref[...], qg_ref[...]) * (d ** -0.5 * math.log2(math.e))
    first = (lax.broadcasted_iota(jnp.int32, q.shape, 1) % (2 * d)) < d
    qz_ref[0] = jnp.where(first, q, 0.0).astype(BF16)
    qz_ref[1] = jnp.where(first, 0.0, q).astype(BF16)
    kn_ref[...] = qk_norm(k_ref[...], kg_ref[...]).astype(BF16)
    vt = v_ref[...].T.astype(BF16)
    e = DIFF_V_DIM
    fill_row = lax.broadcasted_iota(jnp.int32, (VT_ROWS - e, vt.shape[1]), 0)
    fill = jnp.where(fill_row == 0, 1.0, 0.0).astype(BF16)
    for h in range(DIFF_HEADS):
        vt_ref[h * VT_ROWS:h * VT_ROWS + e, :] = vt[h * e:(h + 1) * e]
        vt_ref[h * VT_ROWS + e:(h + 1) * VT_ROWS, :] = fill


def _attn_pre(p, q_gain, k_gain):
    tp = p.shape[0]
    w = DIFF_WIDTH
    vec = pl.BlockSpec((1, w), lambda i: (0, 0))
    blk = lambda j: pl.BlockSpec((ROW_TILE, w), lambda i: (i, j))
    return pl.pallas_call(
        _attn_pre_body,
        grid=(tp // ROW_TILE,),
        in_specs=[blk(0), blk(1), blk(2), vec, vec],
        out_specs=[pl.BlockSpec((2, ROW_TILE, w), lambda i: (0, i, 0)), blk(0),
                   pl.BlockSpec((DIFF_HEADS * VT_ROWS, ROW_TILE), lambda i: (0, i))],
        out_shape=[jax.ShapeDtypeStruct((2, tp, w), BF16), jax.ShapeDtypeStruct((tp, w), BF16),
                   jax.ShapeDtypeStruct((DIFF_HEADS * VT_ROWS, tp), BF16)],
        compiler_params=_cparams(("arbitrary",)),
        name="attn_pre",
    )(p, p, p, q_gain, k_gain)


FLASH_HEADS = 4
VT_ROWS = DIFF_V_DIM + 16
FLASH_QCOLS = 256


def _flash_body(qi_ref, ki_ref, path_ref, qz_ref, k_ref, vt_ref, lam_ref, sw_ref, o_ref, m_ref, acc_ref,
                *, first_key, lambda_init):
    step = pl.program_id(1)
    qi = qi_ref[step]
    ki = ki_ref[step]
    path = path_ref[step]
    tq = o_ref.shape[0]
    tk = k_ref.shape[0]
    e = DIFF_V_DIM
    heads = range(FLASH_HEADS)
    lanes = [slice(g * e, (g + 1) * e) for g in heads]

    @pl.when(ki == 0)
    def _():
        m_ref[...] = jnp.full_like(m_ref, -jnp.inf)
        acc_ref[...] = jnp.zeros_like(acc_ref)

    def strand(g, c0, rows, mask_rows):
        cols = slice(c0, c0 + FLASH_QCOLS)
        which, off = divmod(c0, tq)
        st = {}

        def stage_scores():
            q = qz_ref[which, off:off + FLASH_QCOLS, lanes[g]]
            s = lax.dot_general(k_ref[rows, lanes[g]], q, (((1,), (1,)), ((), ())),
                                preferred_element_type=F32)
            if mask_rows is not None:
                n = mask_rows.stop - mask_rows.start
                k_pos = ki * tk + mask_rows.start + lax.broadcasted_iota(jnp.int32, (n, FLASH_QCOLS), 0)
                q_pos = qi * tq + off + lax.broadcasted_iota(jnp.int32, (n, FLASH_QCOLS), 1)
                vis = (k_pos <= q_pos) & ((k_pos >= first_key) | (q_pos < first_key))
                lo, hi = mask_rows.start - rows.start, mask_rows.stop - rows.start
                parts = [s[:lo], jnp.where(vis, s[lo:hi], -jnp.inf), s[hi:]]
                s = jnp.concatenate([part for part in parts if part.shape[0]], axis=0)
            st["s"] = s

        def stage_softmax():
            m_prev = m_ref[g, :, cols]
            m_new = jnp.maximum(m_prev, jnp.max(st["s"], axis=0, keepdims=True))
            st["alpha"] = jnp.exp2(m_prev - m_new)
            p = jnp.exp2(st["s"] - m_new)
            m_ref[g, :, cols] = m_new
            st["p"] = p.astype(BF16)

        def stage_values():
            acc_ref[g, :, cols] = st["alpha"] * acc_ref[g, :, cols] + jnp.dot(
                vt_ref[g * VT_ROWS:(g + 1) * VT_ROWS, rows], st["p"], preferred_element_type=F32)

        return [stage_scores, stage_softmax, stage_values]

    def run(rows_for):
        strands = [strand(g, c0, *rows_for(c0 % tq))
                   for c0 in range(0, 2 * tq, FLASH_QCOLS) for g in heads]
        depth = len(strands[0])
        for t in range(len(strands) + depth - 1):
            for stage in range(depth):
                if 0 <= t - stage < len(strands):
                    strands[t - stage][stage]()

    tail = first_key // LANES * LANES
    paths = {
        _PATH_DIAG_FIRST_HALF: lambda off: (slice(0, off + FLASH_QCOLS), slice(0, off + FLASH_QCOLS)),
        _PATH_DIAG_SECOND_HALF: lambda off: (slice(0, tq + off + FLASH_QCOLS), slice(tq, tq + off + FLASH_QCOLS)),
        _PATH_PAD_AND_DIAG: lambda off: (slice(tail, tq + off + FLASH_QCOLS), slice(tail, tq + off + FLASH_QCOLS)),
        _PATH_PAD: lambda off: (slice(tail, tk), slice(tail, tq)),
        _PATH_FULL: lambda off: (slice(0, tk), None),
    }
    for pid, rows_for in paths.items():
        pl.when(path == pid)(functools.partial(run, rows_for))

    @pl.when(path <= _PATH_PAD_AND_DIAG)
    def _():
        lp = lam_ref[...]
        lam = (jnp.exp(jnp.sum(lp[0:1] * lp[1:2])) - jnp.exp(jnp.sum(lp[2:3] * lp[3:4])) + lambda_init)
        for g in heads:
            o = acc_ref[g, :e] / acc_ref[g, e:e + 1]
            o = o[:, :tq] - lam * o[:, tq:]
            ms = jnp.mean(o * o, axis=0, keepdims=True)
            o = (o * lax.rsqrt(ms + SUBLN_EPS)).T * (sw_ref[...] * (1.0 - lambda_init))
            o_ref[:, lanes[g]] = o.astype(BF16)


_PATH_DIAG_FIRST_HALF = 0
_PATH_DIAG_SECOND_HALF = 1
_PATH_PAD_AND_DIAG = 2
_PATH_PAD = 3
_PATH_FULL = 4


def _flash(qz, kn, vt, lam_p, subln_w, first_key, lambda_init):
    _, tp, w = qz.shape
    e = DIFF_V_DIM
    we = FLASH_HEADS * e
    tk = 2 * ROW_TILE
    assert first_key < ROW_TILE
    steps = []
    for qi in range(tp // ROW_TILE):
        last = qi // 2
        for ki in range(last + 1):
            if ki == last:
                path = (_PATH_DIAG_FIRST_HALF if qi % 2 == 0 else
                        _PATH_PAD_AND_DIAG if ki == 0 else _PATH_DIAG_SECOND_HALF)
            else:
                path = _PATH_PAD if ki == 0 else _PATH_FULL
            steps.append((qi, ki, path))
    tabs = [jnp.asarray(np.array([s[n] for s in steps], np.int32)) for n in range(3)]
    grid_spec = pltpu.PrefetchScalarGridSpec(
        num_scalar_prefetch=3,
        grid=(DIFF_HEADS // FLASH_HEADS, len(steps)),
        in_specs=[
            pl.BlockSpec((2, ROW_TILE, we), lambda h, s, qt, kt, pt: (0, qt[s], h)),
            pl.BlockSpec((tk, we), lambda h, s, qt, kt, pt: (kt[s], h)),
            pl.BlockSpec((FLASH_HEADS * VT_ROWS, tk), lambda h, s, qt, kt, pt: (h, kt[s])),
            pl.BlockSpec(lam_p.shape, lambda h, s, qt, kt, pt: (0, 0)),
            pl.BlockSpec((1, e), lambda h, s, qt, kt, pt: (0, 0)),
        ],
        out_specs=pl.BlockSpec((ROW_TILE, we), lambda h, s, qt, kt, pt: (qt[s], h)),
        scratch_shapes=[pltpu.VMEM((FLASH_HEADS, 1, 2 * ROW_TILE), F32),
                        pltpu.VMEM((FLASH_HEADS, VT_ROWS, 2 * ROW_TILE), F32)],
    )
    return pl.pallas_call(
        functools.partial(_flash_body, first_key=first_key, lambda_init=lambda_init),
        grid_spec=grid_spec,
        out_shape=jax.ShapeDtypeStruct((tp, w), BF16),
        compiler_params=_cparams(("arbitrary", "arbitrary")),
        name="diff_flash",
    )(*tabs, qz, kn, vt, lam_p, subln_w)


def _out_proj_body(ya_ref, yb_ref, wa_ref, wb_ref, head_ref, x_ref, g_ref, h2_ref, hn_ref, *, head_blocks):
    acc = jnp.dot(ya_ref[...], wa_ref[...], preferred_element_type=F32)
    acc = acc + jnp.dot(yb_ref[...], wb_ref[...], preferred_element_type=F32)
    h2 = jnp.where(pl.program_id(0) < head_blocks, head_ref[...], x_ref[...]) + acc
    h2_ref[...] = h2
    ms = jnp.mean(h2 * h2, axis=-1, keepdims=True)
    hn_ref[...] = (h2 * lax.rsqrt(ms + NORM_EPS) * g_ref[...]).astype(BF16)


def _out_proj(ya, yb, wa, wb, head, x, gain, row_tile):
    d = x.shape[1]
    tp = head.shape[0] + x.shape[0]
    nhb = head.shape[0] // row_tile
    rows = pl.BlockSpec((row_tile, d), lambda i: (i, 0))
    return pl.pallas_call(
        functools.partial(_out_proj_body, head_blocks=nhb),
        grid=(tp // row_tile,),
        in_specs=[
            pl.BlockSpec((row_tile, ya.shape[1]), lambda i: (i, 0)),
            pl.BlockSpec((row_tile, yb.shape[1]), lambda i: (i, 0)),
            pl.BlockSpec(wa.shape, lambda i: (0, 0)),
            pl.BlockSpec(wb.shape, lambda i: (0, 0)),
            pl.BlockSpec((row_tile, d), lambda i: (jnp.minimum(i, nhb - 1), 0)),
            pl.BlockSpec((row_tile, d), lambda i: (jnp.maximum(i - nhb, 0), 0)),
            pl.BlockSpec((1, d), lambda i: (0, 0)),
        ],
        out_specs=[rows, rows],
        out_shape=[jax.ShapeDtypeStruct((tp, d), F32), jax.ShapeDtypeStruct((tp, d), BF16)],
        compiler_params=_cparams(("arbitrary",)),
        name="out_proj",
    )(ya, yb, wa, wb, head, x, gain)


FFN_SUB = 256


def _ffn_body(x_ref, wg_ref, wu_ref, cwg_ref, cwu_ref, cbg_ref, cbu_ref, wd_ref, h2_ref, o_ref,
              acc_ref, ua_ref, ub_ref, *, nf):
    s = pl.program_id(0)
    halo = x_ref.shape[0] - o_ref.shape[0]
    tile = jnp.maximum(s - 1, 0) % nf
    subs = [slice(c, c + FFN_SUB) for c in range(0, wg_ref.shape[1], FFN_SUB)]

    @pl.when(s == 0)
    def _():
        ua_ref[...] = jnp.zeros_like(ua_ref)
        ub_ref[...] = jnp.zeros_like(ub_ref)

    def conv(u, cw_ref, cb_ref, cols):
        cw = cw_ref[:, cols]
        u1 = pltpu.roll(u, 1, axis=0)
        y = pltpu.roll(u1, 1, axis=0) * cw[0:1] + u1 * cw[1:2] + u * cw[2:3]
        return y[halo:] + cb_ref[:, cols]

    def step(u_next, u_prev):
        x = x_ref[...]
        for c in subs:
            u_next[0, :, c] = jnp.dot(x, wg_ref[:, c], preferred_element_type=F32)
            u_next[1, :, c] = jnp.dot(x, wu_ref[:, c], preferred_element_type=F32)
        part = None
        for c in subs:
            gate = conv(u_prev[0, :, c], cwg_ref, cbg_ref, c)
            up = conv(u_prev[1, :, c], cwu_ref, cbu_ref, c)
            act = (gate * _sigmoid(gate) * up).astype(BF16)
            term = jnp.dot(act, wd_ref[c, :], preferred_element_type=F32)
            part = term if part is None else part + term
        acc_ref[...] = jnp.where(tile == 0, part, acc_ref[...] + part)

    @pl.when(s % 2 == 0)
    def _():
        step(ua_ref, ub_ref)

    @pl.when(s % 2 == 1)
    def _():
        step(ub_ref, ua_ref)

    @pl.when((s > 0) & (tile == nf - 1))
    def _():
        o_ref[...] = h2_ref[...] + acc_ref[...]


def _ffn(hn, h2, w_up, conv_w, conv_b, w_down, first_block, n_out_rows, ff_tile):
    tp, d = hn.shape
    dff = w_down.shape[0]
    nf = dff // ff_tile
    halo = 16
    fb = first_block
    items = (n_out_rows // ROW_TILE) * nf

    def up_item(s):
        it = jnp.minimum(s, items - 1)
        return it // nf, it % nf

    def down_item(s):
        it = jnp.maximum(s - 1, 0)
        return it // nf, it % nf

    return pl.pallas_call(
        functools.partial(_ffn_body, nf=nf),
        grid=(items + 1,),
        in_specs=[
            pl.BlockSpec((pl.Element(ROW_TILE + halo), pl.Element(d)),
                         lambda s: (pl.multiple_of((up_item(s)[0] + fb) * ROW_TILE - halo, halo), 0)),
            pl.BlockSpec((d, ff_tile), lambda s: (0, up_item(s)[1])),
            pl.BlockSpec((d, ff_tile), lambda s: (0, up_item(s)[1] + nf)),
            pl.BlockSpec((CONV_WIDTH, ff_tile), lambda s: (0, down_item(s)[1])),
            pl.BlockSpec((CONV_WIDTH, ff_tile), lambda s: (0, down_item(s)[1] + nf)),
            pl.BlockSpec((1, ff_tile), lambda s: (0, down_item(s)[1])),
            pl.BlockSpec((1, ff_tile), lambda s: (0, down_item(s)[1] + nf)),
            pl.BlockSpec((ff_tile, d), lambda s: (down_item(s)[1], 0)),
            pl.BlockSpec((ROW_TILE, d), lambda s: (down_item(s)[0] + fb, 0)),
        ],
        out_specs=pl.BlockSpec((ROW_TILE, d), lambda s: (down_item(s)[0], 0)),
        out_shape=jax.ShapeDtypeStruct((n_out_rows, d), F32),
        scratch_shapes=[pltpu.VMEM((ROW_TILE, d), F32),
                        pltpu.VMEM((2, ROW_TILE + halo, ff_tile), F32),
                        pltpu.VMEM((2, ROW_TILE + halo, ff_tile), F32)],
        compiler_params=_cparams(("arbitrary",)),
        name="conv_glu_ffn",
    )(hn, w_up, w_up, conv_w, conv_w, conv_b, conv_b, w_down, h2)


def _pad_cols(a, width):
    return jnp.pad(a, ((0, 0), (0, width - a.shape[1])))


def _layer(head, x, first_key, lambda_init, norm1_g, w_in, mu, w0, w2, a0, a2, g2, k_k, k_a, r_k, lnx_w, lnx_b,
           q_gain, k_gain, lam_q1, lam_k1, lam_q2, lam_k2, subln_w, w_out, norm2_g, w_up, conv_w, conv_b,
           w_down):
    w = RWKV_WIDTH
    row = lambda a: a.reshape(1, -1)
    n_rkv = 3 * w
    cuts = np.cumsum((n_rkv,) + LORA_SIZES)
    segs = [w_in[:, cuts[-1]:], w_in[:, :n_rkv]]
    segs += [_pad_cols(w_in[:, lo:hi], wd) for lo, hi, wd in zip(cuts[:-1], cuts[1:], LORA_PAD)]
    segs.append(jnp.zeros((w_in.shape[0], PROJ_COLS - sum(s.shape[1] for s in segs)), w_in.dtype))
    w_proj = jnp.concatenate(segs, axis=1).astype(BF16)
    mu_row = row(mu)
    mu_l = jnp.concatenate([_pad_cols(mu_row[:, lo:hi], wd) for lo, hi, wd in zip(cuts[:-1], cuts[1:], LORA_PAD)],
                           axis=1)

    def lora_weight(wt, rows):
        full = jnp.pad(wt, ((0, rows - wt.shape[0]), (0, 0)))
        hi = full.astype(BF16)
        return hi, (full - hi.astype(F32)).astype(BF16)

    p = _proj(head, x, row(norm1_g), w_proj, PROJ_COLS // 3)

    at, rt, bt, kt, v, bht, kht, gct, g, bonus = _rwkv_pre(
        p, mu_row[:, :n_rkv], mu_l, row(w0), lora_weight(w2, LORA_PAD[0]), row(a0),
        lora_weight(a2, LORA_PAD[1]), lora_weight(g2, LORA_PAD[2]), row(k_k), row(k_a), row(r_k))
    y_a = _rwkv_chunks(at, rt, bt, kt, v, bht, kht, gct, g, bonus, row(lnx_w), row(lnx_b))

    tile_heads = lambda a: jnp.tile(row(a), (1, 2 * DIFF_HEADS))
    qz, kn, vt = _attn_pre(p, tile_heads(q_gain), tile_heads(k_gain))
    lam_p = jnp.stack([lam_q1, lam_k1, lam_q2, lam_k2]).astype(F32)
    y_b = _flash(qz, kn, vt, lam_p, row(subln_w), first_key, lambda_init)

    w_out_b = w_out.astype(BF16)
    h2, hn2 = _out_proj(y_a, y_b, w_out_b[:w], w_out_b[w:], head, x, row(norm2_g), ROW_TILE)

    return _ffn(hn2, h2, w_up.astype(BF16), conv_w, row(conv_b), w_down.astype(BF16), 1, x.shape[0], 512)


def kernel(x, meta, norm1_g, w_in, mu, w0, w2, a0, a2, g2, k_k, k_a, r_k, lnx_w, lnx_b, q_gain, k_gain,
           lam_q1, lam_k1, lam_q2, lam_k2, subln_w, w_out, norm2_g, w_up, conv_w, conv_b, w_down):
    batch, seq, d = x.shape
    depth = norm1_g.shape[0]
    assert depth == 1 and seq % ROW_TILE == 0 and meta.shape[0] == N_META
    front = ROW_TILE - N_META
    head = jnp.concatenate([jnp.zeros((front, d), x.dtype), meta.astype(x.dtype)], axis=0)
    lambda_init = 0.8 - 0.6 * math.exp(-0.3 * 0)
    outs = [
        _layer(head, x[bi], front, lambda_init, norm1_g[0], w_in[0], mu[0], w0[0], w2[0], a0[0], a2[0], g2[0],
               k_k[0], k_a[0], r_k[0], lnx_w[0], lnx_b[0], q_gain[0], k_gain[0], lam_q1[0], lam_k1[0],
               lam_q2[0], lam_k2[0], subln_w[0], w_out[0], norm2_g[0], w_up[0], conv_w[0], conv_b[0], w_down[0])
        for bi in range(batch)]
    return outs[0][None] if batch == 1 else jnp.stack(outs, axis=0)
```

```python
import functools
import math

import jax
import jax.numpy as jnp
import numpy as np
from jax import lax
from jax.experimental import pallas as pl
from jax.experimental.pallas import tpu as pltpu

F32 = jnp.float32
BF16 = jnp.bfloat16

N_META = 16
RWKV_HEADS = 16
RWKV_HEAD_DIM = 64
RWKV_WIDTH = RWKV_HEADS * RWKV_HEAD_DIM
DECAY_LORA = 64
AAA_LORA = 64
GATE_LORA = 160
RWKV_LN_EPS = 64e-5
DIFF_HEADS = 8
DIFF_QK_DIM = 64
DIFF_V_DIM = 128
DIFF_WIDTH = DIFF_HEADS * DIFF_V_DIM
CONV_WIDTH = 3
NORM_EPS = 1e-6
SUBLN_EPS = 1e-5

LANES = 128
ROW_TILE = 512
CHUNK = 64
RWKV_ROWS = 4 * CHUNK
RWKV_CHUNK_ROWS = 8 * CHUNK
LORA_SIZES = (DECAY_LORA, AAA_LORA, GATE_LORA)
LORA_PAD = (128, 128, 256)
LORA_COLS = sum(LORA_PAD)
MXU_COLS = 256
PROJ_COLS = -(-(3 * DIFF_WIDTH + 3 * RWKV_WIDTH + LORA_COLS) // (3 * MXU_COLS)) * (3 * MXU_COLS)
VMEM_LIMIT = 56 * 1024 * 1024


def _cparams(sem):
    return pltpu.CompilerParams(dimension_semantics=sem, vmem_limit_bytes=VMEM_LIMIT)


def _split_dot(a, b_bf16, parts):
    acc = None
    rem = a
    for _ in range(parts):
        hi = rem.astype(BF16)
        term = jnp.dot(hi, b_bf16, preferred_element_type=F32)
        acc = term if acc is None else acc + term
        rem = rem - hi.astype(F32)
    return acc


def _dot3(a, b_hi, b_lo):
    ah = a.astype(BF16)
    al = (a - ah.astype(F32)).astype(BF16)
    d = functools.partial(jnp.dot, preferred_element_type=F32)
    return d(ah, b_hi) + (d(ah, b_lo) + d(al, b_hi))


def _group_ones(n, group):
    r = lax.broadcasted_iota(jnp.int32, (n, n), 0) // group
    c = lax.broadcasted_iota(jnp.int32, (n, n), 1) // group
    return (r == c).astype(BF16)


def _group_sum(x, group):
    ones = _group_ones(LANES, group)
    cols = [
        _split_dot(x[:, c:c + LANES], ones, 2)
        for c in range(0, x.shape[1], LANES)
    ]
    return jnp.concatenate(cols, axis=1)


def _proj_body(head_ref, x_ref, g_ref, w_ref, o_ref, hn_ref):
    @pl.when(pl.program_id(1) == 0)
    def _():
        x = jnp.where(pl.program_id(0) == 0, head_ref[...], x_ref[...])
        ms = jnp.mean(x * x, axis=-1, keepdims=True)
        hn_ref[...] = (x * lax.rsqrt(ms + NORM_EPS) * g_ref[...]).astype(BF16)

    o_ref[...] = jnp.dot(hn_ref[...], w_ref[...], preferred_element_type=F32)


def _proj(head, x, gain, w, col_tile):
    d = x.shape[1]
    tp = head.shape[0] + x.shape[0]
    n = w.shape[1]
    return pl.pallas_call(
        _proj_body,
        grid=(tp // ROW_TILE, n // col_tile),
        in_specs=[
            pl.BlockSpec((ROW_TILE, d), lambda i, j: (0, 0)),
            pl.BlockSpec((ROW_TILE, d), lambda i, j: (jnp.maximum(i - 1, 0), 0)),
            pl.BlockSpec((1, d), lambda i, j: (0, 0)),
            pl.BlockSpec((d, col_tile), lambda i, j: (0, j)),
        ],
        out_specs=pl.BlockSpec((ROW_TILE, col_tile), lambda i, j: (i, j)),
        out_shape=jax.ShapeDtypeStruct((tp, n), F32),
        scratch_shapes=[pltpu.VMEM((ROW_TILE, d), BF16)],
        compiler_params=_cparams(("arbitrary", "arbitrary")),
        name="proj",
    )(head, x, gain, w)


def _softplus(z):
    return jnp.maximum(z, 0.0) + jnp.log(1.0 + jnp.exp(-jnp.abs(z)))


def _sigmoid(z):
    return 0.5 * jnp.tanh(0.5 * z) + 0.5


def _rwkv_pre_body(r_ref, k_ref, v_ref, l_ref, mu_ref, mul_ref, w0_ref, w2h_ref, w2l_ref, a0_ref, a2h_ref,
                   a2l_ref, g2h_ref, g2l_ref, kk_ref, ka_ref, rk_ref,
                   at_ref, rt_ref, bt_ref, kt_ref, vo_ref, bht_ref, kht_ref, gct_ref, g_ref, bonus_ref,
                   last_ref, lastl_ref):
    i = pl.program_id(0)

    @pl.when(i == 0)
    def _():
        last_ref[...] = jnp.zeros_like(last_ref)
        lastl_ref[...] = jnp.zeros_like(lastl_ref)

    rows = r_ref.shape[0]
    row = lax.broadcasted_iota(jnp.int32, (rows, 1), 0)

    def shift_mix(p, last_row, mu):
        prev = jnp.where(row == 0, last_row, pltpu.roll(p, 1, axis=0))
        return p + (prev - p) * mu

    w = RWKV_WIDTH
    r_raw, k_raw, v_raw, l_raw = r_ref[...], k_ref[...], v_ref[...], l_ref[...]
    r = shift_mix(r_raw, last_ref[0:1, 0:w], mu_ref[:, 0:w])
    k = shift_mix(k_raw, last_ref[0:1, w:2 * w], mu_ref[:, w:2 * w])
    v = shift_mix(v_raw, last_ref[0:1, 2 * w:3 * w], mu_ref[:, 2 * w:3 * w])
    lo = shift_mix(l_raw, lastl_ref[0:1, :], mul_ref[...])
    last_ref[0:1, 0:w] = r_raw[rows - 1:rows]
    last_ref[0:1, w:2 * w] = k_raw[rows - 1:rows]
    last_ref[0:1, 2 * w:3 * w] = v_raw[rows - 1:rows]
    lastl_ref[0:1, :] = l_raw[rows - 1:rows]

    c0, c1 = LORA_PAD[0], LORA_PAD[0] + LORA_PAD[1]
    wd, ad, gd = lo[:, 0:c0], lo[:, c0:c1], lo[:, c1:]
    wlog = -_softplus(-(w0_ref[...] + _dot3(jnp.tanh(wd), w2h_ref[...], w2l_ref[...]))) - 0.5
    logd = -jnp.exp(wlog)
    a = _sigmoid(a0_ref[...] + _dot3(ad, a2h_ref[...], a2l_ref[...]))
    g_ref[...] = _dot3(_sigmoid(gd), g2h_ref[...], g2l_ref[...])

    kk = k * kk_ref[...]
    kk = kk * jnp.minimum(lax.rsqrt(_group_sum(kk * kk, RWKV_HEAD_DIM)), 1e12)
    k = k * (1.0 + (a - 1.0) * ka_ref[...])
    b = kk * a
    bonus_ref[...] = _group_sum(r * k * rk_ref[...], RWKV_HEAD_DIM) * v

    tr = lax.broadcasted_iota(jnp.int32, (rows, rows), 0)
    tc = lax.broadcasted_iota(jnp.int32, (rows, rows), 1)
    same = (tr // CHUNK) == (tc // CHUNK)
    cum, tot = _split_dot_lhs([(same & (tc <= tr)).astype(BF16), same.astype(BF16)], logd)

    e_prev = jnp.exp(cum - logd)
    e_cum = jnp.exp(cum)
    e_neg = jnp.exp(-cum)
    e_rem = jnp.exp(tot - cum)
    at_ref[...] = (-kk * e_prev).astype(BF16)
    rt_ref[...] = (r * e_cum).astype(BF16)
    bt_ref[...] = (b * e_neg).astype(BF16)
    kt_ref[...] = (k * e_neg).astype(BF16)
    vo_ref[...] = v.astype(BF16)
    bht_ref[...] = (b * e_rem).T.astype(BF16)
    kht_ref[...] = (k * e_rem).T.astype(BF16)
    gct_ref[...] = jnp.exp(tot).T


def _split_dot_lhs(lhs_list, b):
    accs = [None] * len(lhs_list)
    rem = b
    for _ in range(3):
        hi = rem.astype(BF16)
        for n, a in enumerate(lhs_list):
            term = jnp.dot(a, hi, preferred_element_type=F32)
            accs[n] = term if accs[n] is None else accs[n] + term
        rem = rem - hi.astype(F32)
    return accs


def _rwkv_pre(p, mu, mul, w0, w2, a0, a2, g2, k_k, k_a, r_k):
    tp = p.shape[0]
    w = RWKV_WIDTH
    rows = RWKV_ROWS
    first = 3 * DIFF_WIDTH // w
    lora_blk = (3 * DIFF_WIDTH + 3 * w) // LORA_COLS

    def full(arr):
        return pl.BlockSpec(arr.shape, lambda i: (0, 0))

    tok = pl.BlockSpec((rows, w), lambda i: (i, 0))
    chan = pl.BlockSpec((w, rows), lambda i: (0, i))
    tok_bf = jax.ShapeDtypeStruct((tp, w), BF16)
    params = (mu, mul, w0, *w2, a0, *a2, *g2, k_k, k_a, r_k)
    return pl.pallas_call(
        _rwkv_pre_body,
        grid=(tp // rows,),
        in_specs=[
            pl.BlockSpec((rows, w), lambda i: (i, first)),
            pl.BlockSpec((rows, w), lambda i: (i, first + 1)),
            pl.BlockSpec((rows, w), lambda i: (i, first + 2)),
            pl.BlockSpec((rows, LORA_COLS), lambda i: (i, lora_blk)),
            *[full(a) for a in params],
        ],
        out_specs=[tok, tok, tok, tok, tok, chan, chan, chan, tok, tok],
        out_shape=[tok_bf, tok_bf, tok_bf, tok_bf, tok_bf,
                   jax.ShapeDtypeStruct((w, tp), BF16), jax.ShapeDtypeStruct((w, tp), BF16),
                   jax.ShapeDtypeStruct((w, tp), F32),
                   jax.ShapeDtypeStruct((tp, w), F32), jax.ShapeDtypeStruct((tp, w), F32)],
        scratch_shapes=[pltpu.VMEM((8, 3 * w), F32), pltpu.VMEM((8, LORA_COLS), F32)],
        compiler_params=_cparams(("arbitrary",)),
        name="rwkv_pre",
    )(p, p, p, p, *params)


def _rwkv_chunk_body(at_ref, rt_ref, bt_ref, kt_ref, v_ref, bht_ref, kht_ref, gct_ref, g_ref, bonus_ref,
                     lnw_ref, lnb_ref, y_ref, state_ref, yn_ref):
    n = RWKV_HEAD_DIM
    c = CHUNK

    @pl.when(pl.program_id(0) == 0)
    def _():
        state_ref[...] = jnp.zeros_like(state_ref)

    tr = lax.broadcasted_iota(jnp.int32, (2 * c, 2 * c), 0)
    tc = lax.broadcasted_iota(jnp.int32, (2 * c, 2 * c), 1) % c
    mask = ((tr < c) & (tc < tr)) | ((tr >= c) & (tc <= tr - c))
    er = lax.broadcasted_iota(jnp.int32, (c, 2 * c), 0)
    ec = lax.broadcasted_iota(jnp.int32, (c, 2 * c), 1)
    right = ec >= c
    eye2 = (ec == er + c).astype(F32)
    zeros_cn = jnp.zeros((c, n), BF16)
    dot = functools.partial(jnp.dot, preferred_element_type=F32)
    bf = lambda t: t.astype(BF16)

    heads = range(RWKV_HEADS)
    chunks = range(at_ref.shape[0] // c)
    units = [(ci, h) for ci in chunks for h in heads]

    def blk(ref, u):
        return ref[u[0] * c:(u[0] + 1) * c, u[1] * n:(u[1] + 1) * n]

    lhs = {u: jnp.concatenate([blk(at_ref, u), blk(rt_ref, u)], axis=0) for u in units}
    w = {}
    for u in units:
        rhs = jnp.concatenate([blk(bt_ref, u), blk(kt_ref, u)], axis=0)
        wu = lax.dot_general(lhs[u], rhs, (((1,), (1,)), ((), ())), preferred_element_type=F32)
        w[u] = jnp.where(mask, wu, 0.0)
    top = {u: bf(w[u][:c]) for u in units}
    bot = {u: bf(w[u][c:]) for u in units}
    xs = {u: jnp.where(right, eye2, w[u][:c]) for u in units}
    for _ in range(int(math.log2(c))):
        xb = {u: bf(xs[u]) for u in units}
        xs = {u: dot(xb[u][:, :c], xb[u]) + jnp.where(right, xs[u], 0.0) for u in units}
    invb = {u: bf(xs[u]) for u in units}

    for ci in chunks:
        us = [(ci, h) for h in heads]
        h0 = [state_ref[h] for h in heads]
        x0 = [dot(lhs[u], bf(h0[u[1]])) for u in us]
        v = [blk(v_ref, u) for u in us]
        sys_rhs = [x0[h][:c] + dot(top[(ci, h)], jnp.concatenate([zeros_cn, v[h]], axis=0)) for h in heads]
        u_sol = [dot(invb[(ci, h)], jnp.concatenate([zeros_cn, bf(sys_rhs[h])], axis=0)) for h in heads]
        uv = [jnp.concatenate([bf(u_sol[h]), v[h]], axis=0) for h in heads]
        for h in heads:
            rs = slice(ci * c, (ci + 1) * c)
            cs = slice(h * n, (h + 1) * n)
            yn_ref[rs, cs] = x0[h][c:] + dot(bot[(ci, h)], uv[h])
            bkt = jnp.concatenate([bht_ref[cs, rs], kht_ref[cs, rs]], axis=1)
            state_ref[h] = gct_ref[cs, rs] * h0[h] + dot(bkt, uv[h])

    y = yn_ref[...]
    inv_n = 1.0 / n
    d = y - _group_sum(y, n) * inv_n
    var = _group_sum(d * d, n) * inv_n
    yn = d * lax.rsqrt(var + RWKV_LN_EPS)
    y_ref[...] = ((yn * lnw_ref[...] + lnb_ref[...] + bonus_ref[...]) * g_ref[...]).astype(BF16)


def _rwkv_chunks(at, rt, bt, kt, v, bht, kht, gct, g, bonus, lnw, lnb):
    tp, w = at.shape
    rows = RWKV_CHUNK_ROWS
    tok = pl.BlockSpec((rows, w), lambda i: (i, 0))
    chan = pl.BlockSpec((w, rows), lambda i: (0, i))
    vec = pl.BlockSpec((1, w), lambda i: (0, 0))
    return pl.pallas_call(
        _rwkv_chunk_body,
        grid=(tp // rows,),
        in_specs=[tok, tok, tok, tok, tok, chan, chan, chan, tok, tok, vec, vec],
        out_specs=tok,
        out_shape=jax.ShapeDtypeStruct((tp, w), BF16),
        scratch_shapes=[pltpu.VMEM((RWKV_HEADS, RWKV_HEAD_DIM, RWKV_HEAD_DIM), F32),
                        pltpu.VMEM((rows, w), F32)],
        compiler_params=_cparams(("arbitrary",)),
        name="rwkv_chunks",
    )(at, rt, bt, kt, v, bht, kht, gct, g, bonus, lnw, lnb)


def _attn_pre_body(q_ref, k_ref, v_ref, qg_ref, kg_ref, qz_ref, kn_ref, vt_ref):
    d = DIFF_QK_DIM

    def qk_norm(x, gain):
        ms = _group_sum(x * x, d) * (1.0 / d)
        return x * lax.rsqrt(ms + NORM_EPS) * gain

    q = qk_norm(q_ref[...], qg_ref[...]) * (d ** -0.5 * math.log2(math.e))
    first = (lax.broadcasted_iota(jnp.int32, q.shape, 1) % (2 * d)) < d
    qz_ref[0] = jnp.where(first, q, 0.0).astype(BF16)
    qz_ref[1] = jnp.where(first, 0.0, q).astype(BF16)
    kn_ref[...] = qk_norm(k_ref[...], kg_ref[...]).astype(BF16)
    vt = v_ref[...].T.astype(BF16)
    e = DIFF_V_DIM
    fill_row = lax.broadcasted_iota(jnp.int32, (VT_ROWS - e, vt.shape[1]), 0)
    fill = jnp.where(fill_row == 0, 1.0, 0.0).astype(BF16)
    for h in range(DIFF_HEADS):
        vt_ref[h * VT_ROWS:h * VT_ROWS + e, :] = vt[h * e:(h + 1) * e]
        vt_ref[h * VT_ROWS + e:(h + 1) * VT_ROWS, :] = fill


def _attn_pre(p, q_gain, k_gain):
    tp = p.shape[0]
    w = DIFF_WIDTH
    vec = pl.BlockSpec((1, w), lambda i: (0, 0))
    blk = lambda j: pl.BlockSpec((ROW_TILE, w), lambda i: (i, j))
    return pl.pallas_call(
        _attn_pre_body,
        grid=(tp // ROW_TILE,),
        in_specs=[blk(0), blk(1), blk(2), vec, vec],
        out_specs=[pl.BlockSpec((2, ROW_TILE, w), lambda i: (0, i, 0)), blk(0),
                   pl.BlockSpec((DIFF_HEADS * VT_ROWS, ROW_TILE), lambda i: (0, i))],
        out_shape=[jax.ShapeDtypeStruct((2, tp, w), BF16), jax.ShapeDtypeStruct((tp, w), BF16),
                   jax.ShapeDtypeStruct((DIFF_HEADS * VT_ROWS, tp), BF16)],
        compiler_params=_cparams(("arbitrary",)),
        name="attn_pre",
    )(p, p, p, q_gain, k_gain)


FLASH_HEADS = 4
VT_ROWS = DIFF_V_DIM + 16
FLASH_QCOLS = 512


def _flash_body(qi_ref, ki_ref, path_ref, qz_ref, k_ref, vt_ref, lam_ref, sw_ref, o_ref, m_ref, acc_ref,
                *, first_key, lambda_init):
    step = pl.program_id(1)
    qi = qi_ref[step]
    ki = ki_ref[step]
    path = path_ref[step]
    tq = o_ref.shape[0]
    tk = k_ref.shape[0]
    e = DIFF_V_DIM
    heads = range(FLASH_HEADS)
    lanes = [slice(g * e, (g + 1) * e) for g in heads]

    @pl.when(ki == 0)
    def _():
        m_ref[...] = jnp.full_like(m_ref, -jnp.inf)
        acc_ref[...] = jnp.zeros_like(acc_ref)

    def strand(g, c0, rows, mask_rows):
        cols = slice(c0, c0 + FLASH_QCOLS)
        which, off = divmod(c0, tq)
        st = {}

        def stage_scores():
            q = qz_ref[which, off:off + FLASH_QCOLS, lanes[g]]
            s = lax.dot_general(k_ref[rows, lanes[g]], q, (((1,), (1,)), ((), ())),
                                preferred_element_type=F32)
            if mask_rows is not None:
                n = mask_rows.stop - mask_rows.start
                k_pos = ki * tk + mask_rows.start + lax.broadcasted_iota(jnp.int32, (n, FLASH_QCOLS), 0)
                q_pos = qi * tq + off + lax.broadcasted_iota(jnp.int32, (n, FLASH_QCOLS), 1)
                vis = (k_pos <= q_pos) & ((k_pos >= first_key) | (q_pos < first_key))
                lo, hi = mask_rows.start - rows.start, mask_rows.stop - rows.start
                parts = [s[:lo], jnp.where(vis, s[lo:hi], -jnp.inf), s[hi:]]
                s = jnp.concatenate([part for part in parts if part.shape[0]], axis=0)
            st["s"] = s

        def stage_softmax():
            m_prev = m_ref[g, :, cols]
            m_new = jnp.maximum(m_prev, jnp.max(st["s"], axis=0, keepdims=True))
            st["alpha"] = jnp.exp2(m_prev - m_new)
            p = jnp.exp2(st["s"] - m_new)
            m_ref[g, :, cols] = m_new
            st["p"] = p.astype(BF16)

        def stage_values():
            acc_ref[g, :, cols] = st["alpha"] * acc_ref[g, :, cols] + jnp.dot(
                vt_ref[g * VT_ROWS:(g + 1) * VT_ROWS, rows], st["p"], preferred_element_type=F32)

        return [stage_scores, stage_softmax, stage_values]

    def run(rows_for):
        strands = [strand(g, c0, *rows_for(c0 % tq))
                   for c0 in range(0, 2 * tq, FLASH_QCOLS) for g in heads]
        depth = len(strands[0])
        for t in range(len(strands) + depth - 1):
            for stage in range(depth):
                if 0 <= t - stage < len(strands):
                    strands[t - stage][stage]()

    tail = first_key // LANES * LANES
    paths = {
        _PATH_DIAG_FIRST_HALF: lambda off: (slice(0, off + FLASH_QCOLS), slice(0, off + FLASH_QCOLS)),
        _PATH_DIAG_SECOND_HALF: lambda off: (slice(0, tq + off + FLASH_QCOLS), slice(tq, tq + off + FLASH_QCOLS)),
        _PATH_PAD_AND_DIAG: lambda off: (slice(tail, tq + off + FLASH_QCOLS), slice(tail, tq + off + FLASH_QCOLS)),
        _PATH_PAD: lambda off: (slice(tail, tk), slice(tail, tq)),
        _PATH_FULL: lambda off: (slice(0, tk), None),
    }
    for pid, rows_for in paths.items():
        pl.when(path == pid)(functools.partial(run, rows_for))

    @pl.when(path <= _PATH_PAD_AND_DIAG)
    def _():
        lp = lam_ref[...]
        lam = (jnp.exp(jnp.sum(lp[0:1] * lp[1:2])) - jnp.exp(jnp.sum(lp[2:3] * lp[3:4])) + lambda_init)
        for g in heads:
            o = acc_ref[g, :e] / acc_ref[g, e:e + 1]
            o = o[:, :tq] - lam * o[:, tq:]
            ms = jnp.mean(o * o, axis=0, keepdims=True)
            o = (o * lax.rsqrt(ms + SUBLN_EPS)).T * (sw_ref[...] * (1.0 - lambda_init))
            o_ref[:, lanes[g]] = o.astype(BF16)


_PATH_DIAG_FIRST_HALF = 0
_PATH_DIAG_SECOND_HALF = 1
_PATH_PAD_AND_DIAG = 2
_PATH_PAD = 3
_PATH_FULL = 4


def _flash(qz, kn, vt, lam_p, subln_w, first_key, lambda_init):
    _, tp, w = qz.shape
    e = DIFF_V_DIM
    we = FLASH_HEADS * e
    tk = 2 * ROW_TILE
    assert first_key < ROW_TILE
    steps = []
    for qi in range(tp // ROW_TILE):
        last = qi // 2
        for ki in range(last + 1):
            if ki == last:
                path = (_PATH_DIAG_FIRST_HALF if qi % 2 == 0 else
                        _PATH_PAD_AND_DIAG if ki == 0 else _PATH_DIAG_SECOND_HALF)
            else:
                path = _PATH_PAD if ki == 0 else _PATH_FULL
            steps.append((qi, ki, path))
    tabs = [jnp.asarray(np.array([s[n] for s in steps], np.int32)) for n in range(3)]
    grid_spec = pltpu.PrefetchScalarGridSpec(
        num_scalar_prefetch=3,
        grid=(DIFF_HEADS // FLASH_HEADS, len(steps)),
        in_specs=[
            pl.BlockSpec((2, ROW_TILE, we), lambda h, s, qt, kt, pt: (0, qt[s], h)),
            pl.BlockSpec((tk, we), lambda h, s, qt, kt, pt: (kt[s], h)),
            pl.BlockSpec((FLASH_HEADS * VT_ROWS, tk), lambda h, s, qt, kt, pt: (h, kt[s])),
            pl.BlockSpec(lam_p.shape, lambda h, s, qt, kt, pt: (0, 0)),
            pl.BlockSpec((1, e), lambda h, s, qt, kt, pt: (0, 0)),
        ],
        out_specs=pl.BlockSpec((ROW_TILE, we), lambda h, s, qt, kt, pt: (qt[s], h)),
        scratch_shapes=[pltpu.VMEM((FLASH_HEADS, 1, 2 * ROW_TILE), F32),
                        pltpu.VMEM((FLASH_HEADS, VT_ROWS, 2 * ROW_TILE), F32)],
    )
    return pl.pallas_call(
        functools.partial(_flash_body, first_key=first_key, lambda_init=lambda_init),
        grid_spec=grid_spec,
        out_shape=jax.ShapeDtypeStruct((tp, w), BF16),
        compiler_params=_cparams(("arbitrary", "arbitrary")),
        name="diff_flash",
    )(*tabs, qz, kn, vt, lam_p, subln_w)


def _out_proj_body(ya_ref, yb_ref, wa_ref, wb_ref, head_ref, x_ref, g_ref, h2_ref, hn_ref, *, head_blocks):
    acc = jnp.dot(ya_ref[...], wa_ref[...], preferred_element_type=F32)
    acc = acc + jnp.dot(yb_ref[...], wb_ref[...], preferred_element_type=F32)
    h2 = jnp.where(pl.program_id(0) < head_blocks, head_ref[...], x_ref[...]) + acc
    h2_ref[...] = h2
    ms = jnp.mean(h2 * h2, axis=-1, keepdims=True)
    hn_ref[...] = (h2 * lax.rsqrt(ms + NORM_EPS) * g_ref[...]).astype(BF16)


def _out_proj(ya, yb, wa, wb, head, x, gain, row_tile):
    d = x.shape[1]
    tp = head.shape[0] + x.shape[0]
    nhb = head.shape[0] // row_tile
    rows = pl.BlockSpec((row_tile, d), lambda i: (i, 0))
    return pl.pallas_call(
        functools.partial(_out_proj_body, head_blocks=nhb),
        grid=(tp // row_tile,),
        in_specs=[
            pl.BlockSpec((row_tile, ya.shape[1]), lambda i: (i, 0)),
            pl.BlockSpec((row_tile, yb.shape[1]), lambda i: (i, 0)),
            pl.BlockSpec(wa.shape, lambda i: (0, 0)),
            pl.BlockSpec(wb.shape, lambda i: (0, 0)),
            pl.BlockSpec((row_tile, d), lambda i: (jnp.minimum(i, nhb - 1), 0)),
            pl.BlockSpec((row_tile, d), lambda i: (jnp.maximum(i - nhb, 0), 0)),
            pl.BlockSpec((1, d), lambda i: (0, 0)),
        ],
        out_specs=[rows, rows],
        out_shape=[jax.ShapeDtypeStruct((tp, d), F32), jax.ShapeDtypeStruct((tp, d), BF16)],
        compiler_params=_cparams(("arbitrary",)),
        name="out_proj",
    )(ya, yb, wa, wb, head, x, gain)


FFN_SUB = 256


def _ffn_body(x_ref, wg_ref, wu_ref, cwg_ref, cwu_ref, cbg_ref, cbu_ref, wd_ref, h2_ref, o_ref,
              acc_ref, ua_ref, ub_ref, *, nf):
    s = pl.program_id(0)
    halo = x_ref.shape[0] - o_ref.shape[0]
    tile = jnp.maximum(s - 1, 0) % nf
    subs = [slice(c, c + FFN_SUB) for c in range(0, wg_ref.shape[1], FFN_SUB)]

    @pl.when(s == 0)
    def _():
        ua_ref[...] = jnp.zeros_like(ua_ref)
        ub_ref[...] = jnp.zeros_like(ub_ref)

    def conv(u, cw_ref, cb_ref, cols):
        cw = cw_ref[:, cols]
        u1 = pltpu.roll(u, 1, axis=0)
        y = pltpu.roll(u1, 1, axis=0) * cw[0:1] + u1 * cw[1:2] + u * cw[2:3]
        return y[halo:] + cb_ref[:, cols]

    def step(u_next, u_prev):
        x = x_ref[...]
        for c in subs:
            u_next[0, :, c] = jnp.dot(x, wg_ref[:, c], preferred_element_type=F32)
            u_next[1, :, c] = jnp.dot(x, wu_ref[:, c], preferred_element_type=F32)
        part = None
        for c in subs:
            gate = conv(u_prev[0, :, c], cwg_ref, cbg_ref, c)
            up = conv(u_prev[1, :, c], cwu_ref, cbu_ref, c)
            act = (gate * _sigmoid(gate) * up).astype(BF16)
            term = jnp.dot(act, wd_ref[c, :], preferred_element_type=F32)
            part = term if part is None else part + term
        acc_ref[...] = jnp.where(tile == 0, part, acc_ref[...] + part)

    @pl.when(s % 2 == 0)
    def _():
        step(ua_ref, ub_ref)

    @pl.when(s % 2 == 1)
    def _():
        step(ub_ref, ua_ref)

    @pl.when((s > 0) & (tile == nf - 1))
    def _():
        o_ref[...] = h2_ref[...] + acc_ref[...]


def _ffn(hn, h2, w_up, conv_w, conv_b, w_down, first_block, n_out_rows, ff_tile):
    tp, d = hn.shape
    dff = w_down.shape[0]
    nf = dff // ff_tile
    halo = 16
    fb = first_block
    items = (n_out_rows // ROW_TILE) * nf

    def up_item(s):
        it = jnp.minimum(s, items - 1)
        return it // nf, it % nf

    def down_item(s):
        it = jnp.maximum(s - 1, 0)
        return it // nf, it % nf

    return pl.pallas_call(
        functools.partial(_ffn_body, nf=nf),
        grid=(items + 1,),
        in_specs=[
            pl.BlockSpec((pl.Element(ROW_TILE + halo), pl.Element(d)),
                         lambda s: (pl.multiple_of((up_item(s)[0] + fb) * ROW_TILE - halo, halo), 0)),
            pl.BlockSpec((d, ff_tile), lambda s: (0, up_item(s)[1])),
            pl.BlockSpec((d, ff_tile), lambda s: (0, up_item(s)[1] + nf)),
            pl.BlockSpec((CONV_WIDTH, ff_tile), lambda s: (0, down_item(s)[1])),
            pl.BlockSpec((CONV_WIDTH, ff_tile), lambda s: (0, down_item(s)[1] + nf)),
            pl.BlockSpec((1, ff_tile), lambda s: (0, down_item(s)[1])),
            pl.BlockSpec((1, ff_tile), lambda s: (0, down_item(s)[1] + nf)),
            pl.BlockSpec((ff_tile, d), lambda s: (down_item(s)[1], 0)),
            pl.BlockSpec((ROW_TILE, d), lambda s: (down_item(s)[0] + fb, 0)),
        ],
        out_specs=pl.BlockSpec((ROW_TILE, d), lambda s: (down_item(s)[0], 0)),
        out_shape=jax.ShapeDtypeStruct((n_out_rows, d), F32),
        scratch_shapes=[pltpu.VMEM((ROW_TILE, d), F32),
                        pltpu.VMEM((2, ROW_TILE + halo, ff_tile), F32),
                        pltpu.VMEM((2, ROW_TILE + halo, ff_tile), F32)],
        compiler_params=_cparams(("arbitrary",)),
        name="conv_glu_ffn",
    )(hn, w_up, w_up, conv_w, conv_w, conv_b, conv_b, w_down, h2)


def _pad_cols(a, width):
    return jnp.pad(a, ((0, 0), (0, width - a.shape[1])))


def _layer(head, x, first_key, lambda_init, norm1_g, w_in, mu, w0, w2, a0, a2, g2, k_k, k_a, r_k, lnx_w, lnx_b,
           q_gain, k_gain, lam_q1, lam_k1, lam_q2, lam_k2, subln_w, w_out, norm2_g, w_up, conv_w, conv_b,
           w_down):
    w = RWKV_WIDTH
    row = lambda a: a.reshape(1, -1)
    n_rkv = 3 * w
    cuts = np.cumsum((n_rkv,) + LORA_SIZES)
    segs = [w_in[:, cuts[-1]:], w_in[:, :n_rkv]]
    segs += [_pad_cols(w_in[:, lo:hi], wd) for lo, hi, wd in zip(cuts[:-1], cuts[1:], LORA_PAD)]
    segs.append(jnp.zeros((w_in.shape[0], PROJ_COLS - sum(s.shape[1] for s in segs)), w_in.dtype))
    w_proj = jnp.concatenate(segs, axis=1).astype(BF16)
    mu_row = row(mu)
    mu_l = jnp.concatenate([_pad_cols(mu_row[:, lo:hi], wd) for lo, hi, wd in zip(cuts[:-1], cuts[1:], LORA_PAD)],
                           axis=1)

    def lora_weight(wt, rows):
        full = jnp.pad(wt, ((0, rows - wt.shape[0]), (0, 0)))
        hi = full.astype(BF16)
        return hi, (full - hi.astype(F32)).astype(BF16)

    p = _proj(head, x, row(norm1_g), w_proj, PROJ_COLS // 3)

    at, rt, bt, kt, v, bht, kht, gct, g, bonus = _rwkv_pre(
        p, mu_row[:, :n_rkv], mu_l, row(w0), lora_weight(w2, LORA_PAD[0]), row(a0),
        lora_weight(a2, LORA_PAD[1]), lora_weight(g2, LORA_PAD[2]), row(k_k), row(k_a), row(r_k))
    y_a = _rwkv_chunks(at, rt, bt, kt, v, bht, kht, gct, g, bonus, row(lnx_w), row(lnx_b))

    tile_heads = lambda a: jnp.tile(row(a), (1, 2 * DIFF_HEADS))
    qz, kn, vt = _attn_pre(p, tile_heads(q_gain), tile_heads(k_gain))
    lam_p = jnp.stack([lam_q1, lam_k1, lam_q2, lam_k2]).astype(F32)
    y_b = _flash(qz, kn, vt, lam_p, row(subln_w), first_key, lambda_init)

    w_out_b = w_out.astype(BF16)
    h2, hn2 = _out_proj(y_a, y_b, w_out_b[:w], w_out_b[w:], head, x, row(norm2_g), ROW_TILE)

    return _ffn(hn2, h2, w_up.astype(BF16), conv_w, row(conv_b), w_down.astype(BF16), 1, x.shape[0], 512)


def kernel(x, meta, norm1_g, w_in, mu, w0, w2, a0, a2, g2, k_k, k_a, r_k, lnx_w, lnx_b, q_gain, k_gain,
           lam_q1, lam_k1, lam_q2, lam_k2, subln_w, w_out, norm2_g, w_up, conv_w, conv_b, w_down):
    batch, seq, d = x.shape
    depth = norm1_g.shape[0]
    assert depth == 1 and seq % ROW_TILE == 0 and meta.shape[0] == N_META
    front = ROW_TILE - N_META
    head = jnp.concatenate([jnp.zeros((front, d), x.dtype), meta.astype(x.dtype)], axis=0)
    lambda_init = 0.8 - 0.6 * math.exp(-0.3 * 0)
    outs = [
        _layer(head, x[bi], front, lambda_init, norm1_g[0], w_in[0], mu[0], w0[0], w2[0], a0[0], a2[0], g2[0],
               k_k[0], k_a[0], r_k[0], lnx_w[0], lnx_b[0], q_gain[0], k_gain[0], lam_q1[0], lam_k1[0],
               lam_q2[0], lam_k2[0], subln_w[0], w_out[0], norm2_g[0], w_up[0], conv_w[0], conv_b[0], w_down[0])
        for bi in range(batch)]
    return outs[0][None] if batch == 1 else jnp.stack(outs, axis=0)
```
